```python
import math
import jax, jax.numpy as jnp
from jax import lax
import numpy as np

D_MODEL = 1024
BATCH = 4
SEQ = 8192
DEPTH = 1

CHUNK = 64
EPS = 1e-6
DA_HEADS = 4
DA_HEAD_DIM = 64
DA_V_DIM = 2 * DA_HEAD_DIM
DA_QK_COLS = DA_HEADS * 2 * DA_HEAD_DIM
DA_WIDTH = DA_HEADS * DA_V_DIM
Q_BLOCK = 128
SG_GROUPS = 4
SG_BLOCK = 128
SG_GROUP_DIM = 128
SG_WIDTH = SG_GROUPS * SG_GROUP_DIM
N_BRANCHES = 2
IN_COLS = 3 * DA_QK_COLS // 1 - DA_QK_COLS + DA_WIDTH + 2 * SG_WIDTH + N_BRANCHES * D_MODEL
PEER_HEADS = 8
PEER_KEYS = 128
PEER_EXPERTS = PEER_KEYS * PEER_KEYS
PEER_KEY_DIM = 256
PEER_HALF = PEER_KEY_DIM // 2
PEER_TOPK = 16
PEER_TOK_BLOCK = 64

kernel_name = 'hybrid_diffattn_sgu_peer_adaln'


def rmsnorm(x, g):
    xf = x.astype(jnp.float32)
    y = xf * lax.rsqrt(jnp.mean(xf * xf, axis=-1, keepdims=True) + EPS)
    return (y * g.astype(jnp.float32)).astype(x.dtype)


def layernorm(x, g, b):
    xf = x.astype(jnp.float32)
    mu = jnp.mean(xf, axis=-1, keepdims=True)
    xc = xf - mu
    y = xc * lax.rsqrt(jnp.mean(xc * xc, axis=-1, keepdims=True) + EPS)
    return (y * g.astype(jnp.float32) + b.astype(jnp.float32)).astype(x.dtype)


def diff_attention(q, k, v, lam, lam_init, head_g):
    B, S = q.shape[0], q.shape[1]
    nblk = S // Q_BLOCK
    scale = DA_HEAD_DIM ** -0.5
    kt = jnp.transpose(k, (0, 2, 3, 1, 4))
    vt = jnp.transpose(v, (0, 2, 1, 3))
    qb = q.reshape(B, nblk, Q_BLOCK, DA_HEADS, 2, DA_HEAD_DIM).transpose(1, 0, 3, 4, 2, 5)
    key_chunk = jnp.arange(S) // CHUNK

    def block(args):
        qi, bi = args
        s = jnp.einsum('bhiqd,bhikd->bhiqk', qi, kt).astype(jnp.float32) * scale
        q_chunk = (bi * Q_BLOCK + jnp.arange(Q_BLOCK)) // CHUNK
        mask = key_chunk[None, :] <= q_chunk[:, None]
        p = jax.nn.softmax(jnp.where(mask, s, -jnp.inf), axis=-1)
        a = p[:, :, 0] - lam * p[:, :, 1]
        return jnp.einsum('bhqk,bhkv->bqhv', a.astype(vt.dtype), vt)

    o = lax.map(block, (qb, jnp.arange(nblk)))
    o = o.transpose(1, 0, 2, 3, 4).reshape(B, S, DA_HEADS, DA_V_DIM)
    o = rmsnorm(o, head_g) * (1.0 - lam_init)
    return o.reshape(B, S, DA_WIDTH)


def spatial_gating(u, sv, ln_g, ln_b, w_s, b_s):
    B, S = u.shape[0], u.shape[1]
    nb = S // SG_BLOCK
    sv = layernorm(sv, ln_g, ln_b)
    pos_chunk = jnp.arange(SG_BLOCK) // CHUNK
    mask = pos_chunk[:, None] >= pos_chunk[None, :]
    w = jnp.where(mask[None], w_s, 0.0)
    svb = sv.reshape(B, nb, SG_BLOCK, SG_GROUPS, SG_GROUP_DIM)
    mixed = jnp.einsum('gpq,bnqgc->bnpgc', w, svb) + b_s.T[None, None, :, :, None]
    return u * mixed.reshape(B, S, SG_WIDTH)


def peer(h, w_query, sub_keys, down, up):
    B, S, D = h.shape
    q = jnp.einsum('bsd,dk->bsk', h, w_query).reshape(B, S, PEER_HEADS, 2, PEER_HALF)
    sc = jnp.einsum('bshpd,hpnd->bshpn', q, sub_keys).astype(jnp.float32)
    s1, i1 = lax.top_k(sc[:, :, :, 0, :], PEER_TOPK)
    s2, i2 = lax.top_k(sc[:, :, :, 1, :], PEER_TOPK)
    cand = (s1[..., :, None] + s2[..., None, :]).reshape(B, S, PEER_HEADS, PEER_TOPK * PEER_TOPK)
    cidx = (i1[..., :, None] * PEER_KEYS + i2[..., None, :]).reshape(B, S, PEER_HEADS, PEER_TOPK * PEER_TOPK)
    top_s, pos = lax.top_k(cand, PEER_TOPK)
    eidx = jnp.take_along_axis(cidx, pos, axis=-1)
    g = jax.nn.softmax(top_s, axis=-1)
    nb = S // PEER_TOK_BLOCK
    hb = h.reshape(B, nb, PEER_TOK_BLOCK, D).transpose(1, 0, 2, 3)
    eb = eidx.reshape(B, nb, PEER_TOK_BLOCK, PEER_HEADS, PEER_TOPK).transpose(1, 0, 2, 3, 4)
    gb = g.reshape(B, nb, PEER_TOK_BLOCK, PEER_HEADS, PEER_TOPK).transpose(1, 0, 2, 3, 4)

    def block(args):
        hx, ex, gx = args
        u_e = jnp.take(down, ex, axis=0)
        a = jax.nn.gelu(jnp.einsum('btd,bthkd->bthk', hx, u_e).astype(jnp.float32))
        v_e = jnp.take(up, ex, axis=0)
        return jnp.einsum('bthk,bthkd->btd', (gx * a).astype(v_e.dtype), v_e)

    out = lax.map(block, (hb, eb, gb))
    return out.transpose(1, 0, 2, 3).reshape(B, S, D)


def setup_inputs(seed: int = 0) -> dict:
    key = jax.random.key(seed)
    ks = jax.random.split(key, 25)
    L, D = DEPTH, D_MODEL

    def n(k, shape, s):
        return jax.random.normal(k, shape, jnp.float32) * s

    return {
        'x': n(ks[0], (BATCH, SEQ, D), 1.0),
        'c': n(ks[1], (BATCH, D), 1.0),
        'w_ada': n(ks[2], (L, D, 6 * D), D ** -0.5),
        'b_ada': n(ks[3], (L, 6 * D), 0.02),
        'norm1_g': 1.0 + n(ks[4], (L, D), 0.02),
        'w_in': n(ks[5], (L, D, IN_COLS), D ** -0.5),
        'da_lambda_q1': n(ks[6], (L, DA_HEAD_DIM), 0.1),
        'da_lambda_k1': n(ks[7], (L, DA_HEAD_DIM), 0.1),
        'da_lambda_q2': n(ks[8], (L, DA_HEAD_DIM), 0.1),
        'da_lambda_k2': n(ks[9], (L, DA_HEAD_DIM), 0.1),
        'da_head_g': 1.0 + n(ks[10], (L, DA_V_DIM), 0.02),
        'sg_ln_g': 1.0 + n(ks[11], (L, SG_WIDTH), 0.02),
        'sg_ln_b': n(ks[12], (L, SG_WIDTH), 0.02),
        'sg_w': n(ks[13], (L, SG_GROUPS, SG_BLOCK, SG_BLOCK), SG_BLOCK ** -0.5),
        'sg_b': 1.0 + n(ks[14], (L, SG_GROUPS, SG_BLOCK), 0.1),
        'w_branch_a': n(ks[15], (L, DA_WIDTH, D), DA_WIDTH ** -0.5),
        'w_branch_b': n(ks[16], (L, SG_WIDTH, D), SG_WIDTH ** -0.5),
        'w_out': n(ks[17], (L, D, D), D ** -0.5),
        'norm2_g': 1.0 + n(ks[18], (L, D), 0.02),
        'peer_w_query': n(ks[19], (L, D, PEER_HEADS * PEER_KEY_DIM), D ** -0.5),
        'peer_sub_keys': n(ks[20], (L, PEER_HEADS, 2, PEER_KEYS, PEER_HALF), PEER_HALF ** -0.5),
        'peer_down': n(ks[21], (L, PEER_EXPERTS, D), D ** -0.5),
        'peer_up': n(ks[22], (L, PEER_EXPERTS, D), PEER_HEADS ** -0.5),
        'final_g': 1.0 + n(ks[23], (D,), 0.02),
    }


def reference(x, c, w_ada, b_ada, norm1_g, w_in, da_lambda_q1, da_lambda_k1, da_lambda_q2,
              da_lambda_k2, da_head_g, sg_ln_g, sg_ln_b, sg_w, sg_b, w_branch_a, w_branch_b,
              w_out, norm2_g, peer_w_query, peer_sub_keys, peer_down, peer_up, final_g):
    B, S, D = x.shape
    splits = [DA_QK_COLS, 2 * DA_QK_COLS, 2 * DA_QK_COLS + DA_WIDTH,
              2 * DA_QK_COLS + DA_WIDTH + SG_WIDTH, 2 * DA_QK_COLS + DA_WIDTH + 2 * SG_WIDTH]
    for l in range(DEPTH):
        mod = jnp.einsum('bd,de->be', jax.nn.silu(c), w_ada[l]) + b_ada[l]
        sh1, sc1, g1, sh2, sc2, g2 = jnp.split(mod[:, None, :], 6, axis=-1)

        h = rmsnorm(x, norm1_g[l]) * (1.0 + sc1) + sh1
        proj = jnp.einsum('bsd,de->bse', h, w_in[l])
        q, k, v, u, sv, gates = jnp.split(proj, splits, axis=-1)
        q = q.reshape(B, S, DA_HEADS, 2, DA_HEAD_DIM)
        k = k.reshape(B, S, DA_HEADS, 2, DA_HEAD_DIM)
        v = v.reshape(B, S, DA_HEADS, DA_V_DIM)
        lam_init = 0.8 - 0.6 * math.exp(-0.3 * l)
        lam = (jnp.exp(jnp.sum(da_lambda_q1[l].astype(jnp.float32) * da_lambda_k1[l].astype(jnp.float32)))
               - jnp.exp(jnp.sum(da_lambda_q2[l].astype(jnp.float32) * da_lambda_k2[l].astype(jnp.float32)))
               + lam_init)
        ya = diff_attention(q, k, v, lam, lam_init, da_head_g[l])
        yb = spatial_gating(jax.nn.gelu(u), jax.nn.gelu(sv), sg_ln_g[l], sg_ln_b[l], sg_w[l], sg_b[l])
        ga, gb = jnp.split(jax.nn.sigmoid(gates), N_BRANCHES, axis=-1)
        merged = (ga * jnp.einsum('bse,ed->bsd', ya, w_branch_a[l])
                  + gb * jnp.einsum('bse,ed->bsd', yb, w_branch_b[l]))
        x = x + g1 * jnp.einsum('bsd,de->bse', merged, w_out[l])

        h2 = rmsnorm(x, norm2_g[l]) * (1.0 + sc2) + sh2
        x = x + g2 * peer(h2, peer_w_query[l], peer_sub_keys[l], peer_down[l], peer_up[l])
    return rmsnorm(x, final_g)
```

```python
import functools

import jax
import jax.numpy as jnp
from jax import lax
from jax.experimental import pallas as pl
from jax.experimental.pallas import tpu as pltpu

F32 = jnp.float32
BF16 = jnp.bfloat16

EPS = 1e-6
CHUNK = 64
DA_HEADS = 4
DA_HEAD_DIM = 64
DA_V_DIM = 128
DA_QK_COLS = 512
DA_WIDTH = 512
SG_GROUPS = 4
SG_BLOCK = 128
SG_WIDTH = 512
PEER_HEADS = 8
PEER_KEYS = 128
PEER_HALF = 128
PEER_TOPK = 16
LAM_INIT = 0.8 - 0.6 * 1.0

LANES = 128
SUBLANES = 8

PROJ_TM = 512
MERGE_TM = 256
ATTN_TQ = 256
TOPK_TG = 8
PEER_TB = 8
PEER_PAIRS = PEER_HEADS * PEER_TOPK
SLAB = 2 * SUBLANES
PITCH = 24
VMEM_LIMIT = 56 * 1024 * 1024


def _rms(x):
    return x * lax.rsqrt(jnp.mean(x * x, axis=-1, keepdims=True) + EPS)


def _ada_kernel(c_ref, w_ref, b_ref, o_ref):
    c = c_ref[...]
    s = (c * jax.nn.sigmoid(c)).astype(BF16)
    o_ref[...] = jnp.dot(s, w_ref[...].astype(BF16), preferred_element_type=F32) + b_ref[...]


def _ada(c, w, b):
    bsz, d = c.shape
    e = w.shape[1]
    tn = 1536
    return pl.pallas_call(
        _ada_kernel,
        grid=(e // tn,),
        in_specs=[pl.BlockSpec((bsz, d), lambda j: (0, 0)),
                  pl.BlockSpec((d, tn), lambda j: (0, j)),
                  pl.BlockSpec((1, tn), lambda j: (0, j))],
        out_specs=pl.BlockSpec((bsz, tn), lambda j: (0, j)),
        out_shape=jax.ShapeDtypeStruct((bsz, e), F32),
        name="ada",
    )(c, w, b.reshape(1, e))


def _proj_kernel(x_ref, mod_ref, g_ref, w_ref, lng_ref, lnb_ref, sgw_ref, sgb_ref,
                 q_ref, k_ref, v_ref, yb_ref):
    x = x_ref[...]
    sh1 = mod_ref[0, 0:1, :]
    sc1 = mod_ref[0, 1:2, :]
    h = (_rms(x) * g_ref[...] * (1.0 + sc1) + sh1).astype(BF16)

    def seg(a, b):
        return jnp.dot(h, w_ref[:, a:b], preferred_element_type=F32)

    q_ref[...] = (seg(0, 512) * (DA_HEAD_DIM ** -0.5)).astype(BF16)
    k_ref[...] = seg(512, 1024).astype(BF16)
    v_ref[...] = seg(1024, 1536).astype(BF16)
    u = jax.nn.gelu(seg(1536, 2048))
    sv = jax.nn.gelu(seg(2048, 2560))
    mu = jnp.mean(sv, axis=-1, keepdims=True)
    svc = sv - mu
    svn = svc * lax.rsqrt(jnp.mean(svc * svc, axis=-1, keepdims=True) + EPS)
    svn = (svn * lng_ref[...] + lnb_ref[...]).astype(BF16)
    pr = lax.broadcasted_iota(jnp.int32, (SG_BLOCK, SG_BLOCK), 0) // CHUNK
    pc = lax.broadcasted_iota(jnp.int32, (SG_BLOCK, SG_BLOCK), 1) // CHUNK
    keep = pr >= pc
    tm = x.shape[0]
    for g in range(SG_GROUPS):
        wm = jnp.where(keep, sgw_ref[g], 0.0).astype(BF16)
        bias = sgb_ref[g]
        cs = slice(g * SG_BLOCK, (g + 1) * SG_BLOCK)
        for r in range(tm // SG_BLOCK):
            rs = slice(r * SG_BLOCK, (r + 1) * SG_BLOCK)
            mixed = jnp.dot(wm, svn[rs, cs], preferred_element_type=F32) + bias
            yb_ref[rs, cs] = (u[rs, cs] * mixed).astype(BF16)


def _proj(xf, mod6, norm_g, w1, ln_g, ln_b, sg_w, sg_b, seq):
    n, d = xf.shape
    tm = min(PROJ_TM, seq)
    tiles_per_batch = seq // tm
    row = lambda i: (i, 0)
    const2 = lambda i: (0, 0)
    const3 = lambda i: (0, 0, 0)
    out = jax.ShapeDtypeStruct((n, 512), BF16)
    return pl.pallas_call(
        _proj_kernel,
        grid=(n // tm,),
        in_specs=[pl.BlockSpec((tm, d), row),
                  pl.BlockSpec((1, 6, d), lambda i: (i // tiles_per_batch, 0, 0)),
                  pl.BlockSpec((1, d), const2),
                  pl.BlockSpec(w1.shape, const2),
                  pl.BlockSpec((1, SG_WIDTH), const2),
                  pl.BlockSpec((1, SG_WIDTH), const2),
                  pl.BlockSpec((SG_GROUPS, SG_BLOCK, SG_BLOCK), const3),
                  pl.BlockSpec((SG_GROUPS, SG_BLOCK, 1), const3)],
        out_specs=[pl.BlockSpec((tm, 512), row)] * 4,
        out_shape=[out] * 4,
        compiler_params=pltpu.CompilerParams(vmem_limit_bytes=VMEM_LIMIT),
        name="proj",
    )(xf, mod6, norm_g.reshape(1, d), w1, ln_g.reshape(1, -1), ln_b.reshape(1, -1),
      sg_w, sg_b[..., None])


def _attn_kernel(lq1_ref, lk1_ref, lq2_ref, lk2_ref, q_ref, k_ref, v_ref, hg_ref, o_ref,
                 qs_ref, m_ref, l_ref, acc_ref):
    tq = q_ref.shape[1]
    qi = pl.program_id(2)
    q = q_ref[0]
    lane = lax.broadcasted_iota(jnp.int32, q.shape, 1)
    zero = jnp.zeros_like(q)
    qs_ref[0:tq, :] = jnp.where(lane < DA_HEAD_DIM, q, zero)
    qs_ref[tq:2 * tq, :] = jnp.where(lane >= DA_HEAD_DIM, q, zero)
    m_ref[...] = jnp.full(m_ref.shape, -1e30, F32)
    l_ref[...] = jnp.zeros(l_ref.shape, F32)
    acc_ref[...] = jnp.zeros(acc_ref.shape, F32)

    def step(j, masked):
        kt = k_ref[0, pl.ds(pl.multiple_of(j * tq, tq), tq), :]
        vt = v_ref[0, pl.ds(pl.multiple_of(j * tq, tq), tq), :]
        s = lax.dot_general(qs_ref[...], kt, (((1,), (1,)), ((), ())),
                            preferred_element_type=F32)
        if masked:
            qc = (lax.broadcasted_iota(jnp.int32, s.shape, 0) % tq) // CHUNK
            kc = lax.broadcasted_iota(jnp.int32, s.shape, 1) // CHUNK
            s = jnp.where(kc <= qc, s, -1e30)
        m_old = m_ref[...]
        m_new = jnp.maximum(m_old, jnp.max(s, axis=-1, keepdims=True))
        alpha = jnp.exp(m_old - m_new)
        p = jnp.exp(s - m_new)
        l_ref[...] = alpha * l_ref[...] + jnp.sum(p, axis=-1, keepdims=True)
        acc_ref[...] = alpha * acc_ref[...] + jnp.dot(p.astype(BF16), vt,
                                                      preferred_element_type=F32)
        m_ref[...] = m_new

    def body(j, carry):
        step(j, False)
        return carry

    lax.fori_loop(0, qi, body, 0)
    step(qi, True)

    lam = (jnp.exp(jnp.sum(lq1_ref[...] * lk1_ref[...], axis=-1, keepdims=True))
           - jnp.exp(jnp.sum(lq2_ref[...] * lk2_ref[...], axis=-1, keepdims=True)) + LAM_INIT)
    o = acc_ref[0:tq, :] / l_ref[0:tq, :] - lam * (acc_ref[tq:2 * tq, :] / l_ref[tq:2 * tq, :])
    o = _rms(o) * hg_ref[...] * (1.0 - LAM_INIT)
    o_ref[0] = o.astype(BF16)


def _attn(q, k, v, lq1, lk1, lq2, lk2, head_g):
    bsz, seq, _ = q.shape
    tq = min(ATTN_TQ, seq)
    lam_spec = pl.BlockSpec((1, DA_HEAD_DIM), lambda b, h, i: (0, 0))
    return pl.pallas_call(
        _attn_kernel,
        grid=(bsz, DA_HEADS, seq // tq),
        in_specs=[lam_spec, lam_spec, lam_spec, lam_spec,
                  pl.BlockSpec((1, tq, LANES), lambda b, h, i: (b, i, h)),
                  pl.BlockSpec((1, seq, LANES), lambda b, h, i: (b, 0, h)),
                  pl.BlockSpec((1, seq, LANES), lambda b, h, i: (b, 0, h)),
                  pl.BlockSpec((1, DA_V_DIM), lambda b, h, i: (0, 0))],
        out_specs=pl.BlockSpec((1, tq, LANES), lambda b, h, i: (b, i, h)),
        out_shape=jax.ShapeDtypeStruct((bsz, seq, DA_WIDTH), BF16),
        scratch_shapes=[pltpu.VMEM((2 * tq, LANES), BF16),
                        pltpu.VMEM((2 * tq, 1), F32),
                        pltpu.VMEM((2 * tq, 1), F32),
                        pltpu.VMEM((2 * tq, DA_V_DIM), F32)],
        compiler_params=pltpu.CompilerParams(vmem_limit_bytes=VMEM_LIMIT),
        name="attn",
    )(lq1.reshape(1, -1), lk1.reshape(1, -1), lq2.reshape(1, -1), lk2.reshape(1, -1),
      q, k, v, head_g.reshape(1, -1))


def _merge_kernel(x_ref, mod_ref, g1n_ref, ya_ref, yb_ref, wg_ref, wa_ref, wb_ref, wo_ref,
                  g2n_ref, wq_ref, keys_ref, x1_ref, h2_ref, sct_ref):
    x = x_ref[...]
    sh1 = mod_ref[0, 0:1, :]
    sc1 = mod_ref[0, 1:2, :]
    g1 = mod_ref[0, 2:3, :]
    sh2 = mod_ref[0, 3:4, :]
    sc2 = mod_ref[0, 4:5, :]
    d = x.shape[1]
    h = (_rms(x) * g1n_ref[...] * (1.0 + sc1) + sh1).astype(BF16)
    ga = jax.nn.sigmoid(jnp.dot(h, wg_ref[:, 0:d], preferred_element_type=F32))
    a = jnp.dot(ya_ref[...], wa_ref[...], preferred_element_type=F32)
    merged = ga * a
    gb = jax.nn.sigmoid(jnp.dot(h, wg_ref[:, d:2 * d], preferred_element_type=F32))
    b = jnp.dot(yb_ref[...], wb_ref[...], preferred_element_type=F32)
    merged = (merged + gb * b).astype(BF16)
    x1 = x + g1 * jnp.dot(merged, wo_ref[...], preferred_element_type=F32)
    x1_ref[...] = x1
    h2 = _rms(x1) * g2n_ref[...] * (1.0 + sc2) + sh2
    h2_ref[...] = h2
    h2b = h2.astype(BF16)
    for hp in range(2 * PEER_HEADS):
        cs = slice(hp * PEER_HALF, (hp + 1) * PEER_HALF)
        qp = jnp.dot(h2b, wq_ref[:, cs], preferred_element_type=F32).astype(BF16)
        sct_ref[hp] = lax.dot_general(keys_ref[hp], qp, (((1,), (1,)), ((), ())),
                                      preferred_element_type=F32)


def _merge(xf, mod6, norm1_g, ya, yb, wg, wa, wb, wo, norm2_g, wq, keys, seq):
    n, d = xf.shape
    tm = min(MERGE_TM, seq)
    tiles_per_batch = seq // tm
    row = lambda i: (i, 0)
    const2 = lambda i: (0, 0)
    return pl.pallas_call(
        _merge_kernel,
        grid=(n // tm,),
        in_specs=[pl.BlockSpec((tm, d), row),
                  pl.BlockSpec((1, 6, d), lambda i: (i // tiles_per_batch, 0, 0)),
                  pl.BlockSpec((1, d), const2),
                  pl.BlockSpec((tm, DA_WIDTH), row),
                  pl.BlockSpec((tm, SG_WIDTH), row),
                  pl.BlockSpec(wg.shape, const2),
                  pl.BlockSpec(wa.shape, const2),
                  pl.BlockSpec(wb.shape, const2),
                  pl.BlockSpec(wo.shape, const2),
                  pl.BlockSpec((1, d), const2),
                  pl.BlockSpec(wq.shape, const2),
                  pl.BlockSpec(keys.shape, lambda i: (0, 0, 0))],
        out_specs=[pl.BlockSpec((tm, d), row),
                   pl.BlockSpec((tm, d), row),
                   pl.BlockSpec((2 * PEER_HEADS, PEER_KEYS, tm), lambda i: (0, 0, i))],
        out_shape=[jax.ShapeDtypeStruct((n, d), F32),
                   jax.ShapeDtypeStruct((n, d), F32),
                   jax.ShapeDtypeStruct((2 * PEER_HEADS, PEER_KEYS, n), F32)],
        compiler_params=pltpu.CompilerParams(vmem_limit_bytes=VMEM_LIMIT),
        name="merge",
    )(xf, mod6, norm1_g.reshape(1, d), ya, yb, wg, wa, wb, wo, norm2_g.reshape(1, d), wq, keys)


_CAND = [(i, j) for i in range(PEER_TOPK) for j in range(PEER_TOPK) if (i + 1) * (j + 1) <= PEER_TOPK]


def _topk_kernel(s_ref, e_ref, g_ref, w_ref, v_ref, i_ref):
    neg = jnp.float32(-jnp.inf)
    kio = lax.broadcasted_iota(jnp.int32, w_ref.shape, 0)

    for half in range(2):
        w_ref[...] = s_ref[0, half]

        def body(it, carry):
            s = w_ref[...]
            m = jnp.max(s, axis=0)
            idx = jnp.min(jnp.where(s == m[None], kio, PEER_KEYS), axis=0)
            w_ref[...] = jnp.where(kio == idx[None], neg, s)
            v_ref[half, it] = m
            i_ref[half, it] = idx
            return carry

        lax.fori_loop(0, PEER_TOPK, body, 0)

    cand = [v_ref[0, i] + v_ref[1, j] for (i, j) in _CAND]
    cid = [i_ref[0, i] * PEER_KEYS + i_ref[1, j] for (i, j) in _CAND]
    flat = [i * PEER_TOPK + j for (i, j) in _CAND]
    big = PEER_TOPK * PEER_TOPK
    tops = []
    for it in range(PEER_TOPK):
        m = cand[0]
        for c in cand[1:]:
            m = jnp.maximum(m, c)
        pos = jnp.full(m.shape, big, jnp.int32)
        for c, f in zip(cand, flat):
            pos = jnp.minimum(pos, jnp.where(c == m, f, big))
        e = jnp.zeros(m.shape, jnp.int32)
        for n, f in enumerate(flat):
            sel = pos == f
            e = jnp.where(sel, cid[n], e)
            cand[n] = jnp.where(sel, neg, cand[n])
        tops.append(m)
        e_ref[0, it] = e
    ex = [jnp.exp(t - tops[0]) for t in tops]
    tot = ex[0]
    for t in ex[1:]:
        tot = tot + t
    for it in range(PEER_TOPK):
        g_ref[0, it] = ex[it] / tot


def _topk(sct5):
    heads, _, keys, ng, _ = sct5.shape
    tg = min(TOPK_TG, ng)
    blk = (1, PEER_TOPK, tg, LANES)
    return pl.pallas_call(
        _topk_kernel,
        grid=(heads, ng // tg),
        in_specs=[pl.BlockSpec((1, 2, keys, tg, LANES), lambda h, t: (h, 0, 0, t, 0))],
        out_specs=[pl.BlockSpec(blk, lambda h, t: (h, 0, t, 0)),
                   pl.BlockSpec(blk, lambda h, t: (h, 0, t, 0))],
        out_shape=[jax.ShapeDtypeStruct((heads, PEER_TOPK, ng, LANES), jnp.int32),
                   jax.ShapeDtypeStruct((heads, PEER_TOPK, ng, LANES), F32)],
        scratch_shapes=[pltpu.VMEM((keys, tg, LANES), F32),
                        pltpu.VMEM((2, PEER_TOPK, tg, LANES), F32),
                        pltpu.VMEM((2, PEER_TOPK, tg, LANES), jnp.int32)],
        compiler_params=pltpu.CompilerParams(vmem_limit_bytes=VMEM_LIMIT),
        name="topk",
    )(sct5)


_SLOT_ROWS = PEER_TB * PEER_PAIRS * PITCH


def _peer_kernel(idx0_ref, idxn_ref, tab_ref, gate_ref, h2_ref, x1_ref, g2_ref, fg_ref, o_ref,
                 buf_ref, sem_ref):
    i = pl.program_id(0)
    nsteps = pl.num_programs(0)
    slot = i % 2

    def slab_copy(idx_ref, t, j, dst_slot):
        e = idx_ref[t, j]
        row = dst_slot * _SLOT_ROWS + (t * PEER_PAIRS + j) * PITCH
        return pltpu.make_async_copy(tab_ref.at[e], buf_ref.at[pl.ds(row, SLAB), :],
                                     sem_ref.at[dst_slot])

    def issue(idx_ref, t, dst_slot):
        for j in range(PEER_PAIRS):
            slab_copy(idx_ref, t, j, dst_slot).start(priority=j % 2)

    @pl.when(i == 0)
    def _():
        for t in range(PEER_TB):
            issue(idx0_ref, t, 0)

    @pl.when(i + 1 < nsteps)
    def _():
        for t in range(PEER_TB):
            issue(idxn_ref, t, 1 - slot)

    nrows = PEER_TB * PEER_PAIRS * SLAB
    base0 = slot * _SLOT_ROWS
    pltpu.make_async_copy(buf_ref.at[pl.ds(base0, nrows), :], buf_ref.at[pl.ds(base0, nrows), :],
                          sem_ref.at[slot]).wait()

    a_cols = []
    for t in range(PEER_TB):
        base = base0 + t * PEER_PAIRS * PITCH
        acc = jnp.zeros((PEER_PAIRS, LANES), F32)
        for s in range(SUBLANES):
            xs = buf_ref[pl.ds(base + s, PEER_PAIRS, stride=PITCH), :]
            acc = acc + xs * h2_ref[t, s:s + 1, :]
        a_cols.append(jnp.sum(acc, axis=1, keepdims=True))
    a = jnp.concatenate(a_cols, axis=1)
    w = gate_ref[0] * jax.nn.gelu(a)
    g2 = g2_ref[0]
    fg = fg_ref[...]
    for t in range(PEER_TB):
        base = base0 + t * PEER_PAIRS * PITCH
        wb = jnp.broadcast_to(w[:, t:t + 1], (PEER_PAIRS, LANES))
        rows = []
        for s in range(SUBLANES):
            us = buf_ref[pl.ds(base + SUBLANES + s, PEER_PAIRS, stride=PITCH), :]
            rows.append(jnp.sum(us * wb, axis=0, keepdims=True))
        y = jnp.concatenate(rows, axis=0)
        x2 = x1_ref[t] + g2 * y
        ms = jnp.sum(jnp.sum(x2 * x2, axis=1, keepdims=True), axis=0, keepdims=True)
        o_ref[t] = x2 * lax.rsqrt(ms / (SUBLANES * LANES) + EPS) * fg


def _peer(eidx, gate_t, tab, h2r, x1r, g2r, fgr, seq):
    n = eidx.shape[0]
    tb = PEER_TB
    nsteps = n // tb
    steps_per_batch = seq // tb
    tok = lambda i: (i, 0, 0)
    return pl.pallas_call(
        _peer_kernel,
        grid=(nsteps,),
        in_specs=[pl.BlockSpec((tb, PEER_PAIRS), lambda i: (0, 0), memory_space=pltpu.SMEM),
                  pl.BlockSpec((tb, PEER_PAIRS), lambda i: (jnp.minimum(i + 1, nsteps - 1), 0),
                               memory_space=pltpu.SMEM),
                  pl.BlockSpec(memory_space=pl.ANY),
                  pl.BlockSpec((1, PEER_PAIRS, tb), tok),
                  pl.BlockSpec((tb, SUBLANES, LANES), tok),
                  pl.BlockSpec((tb, SUBLANES, LANES), tok),
                  pl.BlockSpec((1, SUBLANES, LANES), lambda i: (i // steps_per_batch, 0, 0)),
                  pl.BlockSpec((SUBLANES, LANES), lambda i: (0, 0))],
        out_specs=pl.BlockSpec((tb, SUBLANES, LANES), tok),
        out_shape=jax.ShapeDtypeStruct((n, SUBLANES, LANES), F32),
        scratch_shapes=[pltpu.VMEM((2 * _SLOT_ROWS, LANES), F32),
                        pltpu.SemaphoreType.DMA((2,))],
        compiler_params=pltpu.CompilerParams(dimension_semantics=("arbitrary",),
                                             vmem_limit_bytes=VMEM_LIMIT),
        name="peer",
    )(eidx, eidx, tab, gate_t, h2r, x1r, g2r, fgr)


def kernel(x, c, w_ada, b_ada, norm1_g, w_in, da_lambda_q1, da_lambda_k1, da_lambda_q2, da_lambda_k2, da_head_g, sg_ln_g, sg_ln_b, sg_w, sg_b, w_branch_a, w_branch_b, w_out, norm2_g, peer_w_query, peer_sub_keys, peer_down, peer_up, final_g):
    bsz, seq, d = x.shape
    n = bsz * seq
    assert w_ada.shape[0] == 1 and d == SUBLANES * LANES
    xf = x.reshape(n, d)

    mod6 = _ada(c, w_ada[0], b_ada[0]).reshape(bsz, 6, d)

    nproj = 2 * DA_QK_COLS + DA_WIDTH + 2 * SG_WIDTH
    w1 = w_in[0][:, :nproj].astype(BF16)
    wg = w_in[0][:, nproj:].astype(BF16)
    q, k, v, yb = _proj(xf, mod6, norm1_g[0], w1, sg_ln_g[0], sg_ln_b[0], sg_w[0], sg_b[0], seq)

    ya = _attn(q.reshape(bsz, seq, -1), k.reshape(bsz, seq, -1), v.reshape(bsz, seq, -1),
               da_lambda_q1[0], da_lambda_k1[0], da_lambda_q2[0], da_lambda_k2[0], da_head_g[0])

    keys = peer_sub_keys[0].reshape(2 * PEER_HEADS, PEER_KEYS, PEER_HALF).astype(BF16)
    x1, h2, sct = _merge(xf, mod6, norm1_g[0], ya.reshape(n, -1), yb, wg,
                         w_branch_a[0].astype(BF16), w_branch_b[0].astype(BF16),
                         w_out[0].astype(BF16), norm2_g[0], peer_w_query[0].astype(BF16), keys, seq)

    sct5 = sct.reshape(PEER_HEADS, 2, PEER_KEYS, n // LANES, LANES)
    eidx_t, gate = _topk(sct5)
    eidx = eidx_t.reshape(PEER_PAIRS, n).T
    gate_t = gate.reshape(PEER_PAIRS, n // PEER_TB, PEER_TB).transpose(1, 0, 2)

    tab = jnp.concatenate([peer_down[0], peer_up[0]], axis=1).reshape(-1, SLAB, LANES)
    out = _peer(eidx, gate_t, tab,
                h2.reshape(n, SUBLANES, LANES), x1.reshape(n, SUBLANES, LANES),
                mod6[:, 5].reshape(bsz, SUBLANES, LANES), final_g.reshape(SUBLANES, LANES), seq)
    return out.reshape(bsz, seq, d)
```

```python
import functools

import jax
import jax.numpy as jnp
from jax import lax
from jax.experimental import pallas as pl
from jax.experimental.pallas import tpu as pltpu

F32 = jnp.float32
BF16 = jnp.bfloat16

EPS = 1e-6
CHUNK = 64
DA_HEADS = 4
DA_HEAD_DIM = 64
DA_V_DIM = 128
DA_QK_COLS = 512
DA_WIDTH = 512
SG_GROUPS = 4
SG_BLOCK = 128
SG_WIDTH = 512
PEER_HEADS = 8
PEER_KEYS = 128
PEER_HALF = 128
PEER_TOPK = 16
LAM_INIT = 0.8 - 0.6 * 1.0

LANES = 128
SUBLANES = 8

PROJ_TM = 512
MERGE_TM = 256
ATTN_TQ = 512
ATTN_RB = 256
TOPK_TG = 8
PEER_TB = 8
PEER_PAIRS = PEER_HEADS * PEER_TOPK
SLAB = 2 * SUBLANES
VMEM_LIMIT = 56 * 1024 * 1024


def _rms(x):
    return x * lax.rsqrt(jnp.mean(x * x, axis=-1, keepdims=True) + EPS)


def _ada_kernel(c_ref, w_ref, b_ref, o_ref):
    c = c_ref[...]
    s = (c * jax.nn.sigmoid(c)).astype(BF16)
    o_ref[...] = jnp.dot(s, w_ref[...].astype(BF16), preferred_element_type=F32) + b_ref[...]


def _ada(c, w, b):
    bsz, d = c.shape
    e = w.shape[1]
    tn = 1536
    return pl.pallas_call(
        _ada_kernel,
        grid=(e // tn,),
        in_specs=[pl.BlockSpec((bsz, d), lambda j: (0, 0)),
                  pl.BlockSpec((d, tn), lambda j: (0, j)),
                  pl.BlockSpec((1, tn), lambda j: (0, j))],
        out_specs=pl.BlockSpec((bsz, tn), lambda j: (0, j)),
        out_shape=jax.ShapeDtypeStruct((bsz, e), F32),
        name="ada",
    )(c, w, b.reshape(1, e))


def _proj_kernel(x_ref, mod_ref, g_ref, w_ref, lng_ref, lnb_ref, sgw_ref, sgb_ref,
                 q_ref, k_ref, v_ref, yb_ref):
    x = x_ref[...]
    sh1 = mod_ref[0, 0:1, :]
    sc1 = mod_ref[0, 1:2, :]
    h = (_rms(x) * g_ref[...] * (1.0 + sc1) + sh1).astype(BF16)

    def seg(a, b):
        return jnp.dot(h, w_ref[:, a:b], preferred_element_type=F32)

    q_ref[...] = (seg(0, 512) * (DA_HEAD_DIM ** -0.5)).astype(BF16)
    k_ref[...] = seg(512, 1024).astype(BF16)
    vv = seg(1024, 1536).astype(BF16)
    ones = jnp.ones((vv.shape[0], DA_V_DIM), BF16)
    for hh in range(DA_HEADS):
        v_ref[:, 2 * hh * DA_V_DIM:(2 * hh + 1) * DA_V_DIM] = vv[:, hh * DA_V_DIM:(hh + 1) * DA_V_DIM]
        v_ref[:, (2 * hh + 1) * DA_V_DIM:(2 * hh + 2) * DA_V_DIM] = ones
    u = jax.nn.gelu(seg(1536, 2048))
    sv = jax.nn.gelu(seg(2048, 2560))
    mu = jnp.mean(sv, axis=-1, keepdims=True)
    svc = sv - mu
    svn = svc * lax.rsqrt(jnp.mean(svc * svc, axis=-1, keepdims=True) + EPS)
    svn = (svn * lng_ref[...] + lnb_ref[...]).astype(BF16)
    pr = lax.broadcasted_iota(jnp.int32, (SG_BLOCK, SG_BLOCK), 0) // CHUNK
    pc = lax.broadcasted_iota(jnp.int32, (SG_BLOCK, SG_BLOCK), 1) // CHUNK
    keep = pr >= pc
    tm = x.shape[0]
    for g in range(SG_GROUPS):
        wm = jnp.where(keep, sgw_ref[g], 0.0).astype(BF16)
        bias = sgb_ref[g]
        cs = slice(g * SG_BLOCK, (g + 1) * SG_BLOCK)
        for r in range(tm // SG_BLOCK):
            rs = slice(r * SG_BLOCK, (r + 1) * SG_BLOCK)
            mixed = jnp.dot(wm, svn[rs, cs], preferred_element_type=F32) + bias
            yb_ref[rs, cs] = (u[rs, cs] * mixed).astype(BF16)


def _proj(xf, mod6, norm_g, w1, ln_g, ln_b, sg_w, sg_b, seq):
    n, d = xf.shape
    tm = min(PROJ_TM, seq)
    tiles_per_batch = seq // tm
    row = lambda i: (i, 0)
    const2 = lambda i: (0, 0)
    const3 = lambda i: (0, 0, 0)
    out = jax.ShapeDtypeStruct((n, 512), BF16)
    return pl.pallas_call(
        _proj_kernel,
        grid=(n // tm,),
        in_specs=[pl.BlockSpec((tm, d), row),
                  pl.BlockSpec((1, 6, d), lambda i: (i // tiles_per_batch, 0, 0)),
                  pl.BlockSpec((1, d), const2),
                  pl.BlockSpec(w1.shape, const2),
                  pl.BlockSpec((1, SG_WIDTH), const2),
                  pl.BlockSpec((1, SG_WIDTH), const2),
                  pl.BlockSpec((SG_GROUPS, SG_BLOCK, SG_BLOCK), const3),
                  pl.BlockSpec((SG_GROUPS, SG_BLOCK, 1), const3)],
        out_specs=[pl.BlockSpec((tm, 512), row), pl.BlockSpec((tm, 512), row),
                   pl.BlockSpec((tm, 2 * DA_WIDTH), row), pl.BlockSpec((tm, 512), row)],
        out_shape=[out, out, jax.ShapeDtypeStruct((n, 2 * DA_WIDTH), BF16), out],
        compiler_params=pltpu.CompilerParams(vmem_limit_bytes=VMEM_LIMIT),
        name="proj",
    )(xf, mod6, norm_g.reshape(1, d), w1, ln_g.reshape(1, -1), ln_b.reshape(1, -1),
      sg_w, sg_b[..., None])


def _attn_kernel(lq1_ref, lk1_ref, lq2_ref, lk2_ref, q_ref, k_ref, v_ref, hg_ref, o_ref,
                 qs_ref, m_ref, acc_ref):
    tq = q_ref.shape[1]
    qi = pl.program_id(2)
    q = q_ref[0]
    lane = lax.broadcasted_iota(jnp.int32, q.shape, 1)
    zero = jnp.zeros_like(q)
    qs_ref[0:tq, :] = jnp.where(lane < DA_HEAD_DIM, q, zero)
    qs_ref[tq:2 * tq, :] = jnp.where(lane >= DA_HEAD_DIM, q, zero)
    m_ref[...] = jnp.full(m_ref.shape, -1e30, F32)
    acc_ref[...] = jnp.zeros(acc_ref.shape, F32)
    rb_rows = min(ATTN_RB, tq)
    nlc = tq // LANES

    def step(j, masked):
        off = pl.multiple_of(j * tq, tq)
        kt = k_ref[0, pl.ds(off, tq), :]
        vt = v_ref[0, pl.ds(off, tq), :]
        for rb in range(2 * tq // rb_rows):
            rs = slice(rb * rb_rows, (rb + 1) * rb_rows)
            s = lax.dot_general(qs_ref[rs, :], kt, (((1,), (1,)), ((), ())),
                                preferred_element_type=F32)
            if masked:
                qpos = (rb * rb_rows) % tq + lax.broadcasted_iota(jnp.int32, s.shape, 0)
                kpos = lax.broadcasted_iota(jnp.int32, s.shape, 1)
                s = jnp.where(kpos // CHUNK <= qpos // CHUNK, s, -1e30)
            m_old = m_ref[rs, :]
            m_new = jnp.maximum(m_old, jnp.max(s, axis=-1, keepdims=True))
            alpha = jnp.exp(m_old - m_new)
            p = jnp.exp(s - jnp.concatenate([m_new] * nlc, axis=1)).astype(BF16)
            pv = jnp.dot(p, vt, preferred_element_type=F32)
            acc_ref[rs, :] = acc_ref[rs, :] * jnp.concatenate([alpha, alpha], axis=1) + pv
            m_ref[rs, :] = m_new

    def body(j, carry):
        step(j, False)
        return carry

    lax.fori_loop(0, qi, body, 0)
    step(qi, True)

    lam = (jnp.exp(jnp.sum(lq1_ref[...] * lk1_ref[...], axis=-1, keepdims=True))
           - jnp.exp(jnp.sum(lq2_ref[...] * lk2_ref[...], axis=-1, keepdims=True)) + LAM_INIT)
    o1 = acc_ref[0:tq, 0:DA_V_DIM] / acc_ref[0:tq, DA_V_DIM:2 * DA_V_DIM]
    o2 = acc_ref[tq:2 * tq, 0:DA_V_DIM] / acc_ref[tq:2 * tq, DA_V_DIM:2 * DA_V_DIM]
    o = o1 - lam * o2
    o = _rms(o) * hg_ref[...] * (1.0 - LAM_INIT)
    o_ref[0] = o.astype(BF16)


def _attn(q, k, v, lq1, lk1, lq2, lk2, head_g):
    bsz, seq, _ = q.shape
    tq = min(ATTN_TQ, seq)
    lam_spec = pl.BlockSpec((1, DA_HEAD_DIM), lambda b, h, i: (0, 0))
    return pl.pallas_call(
        _attn_kernel,
        grid=(bsz, DA_HEADS, seq // tq),
        in_specs=[lam_spec, lam_spec, lam_spec, lam_spec,
                  pl.BlockSpec((1, tq, LANES), lambda b, h, i: (b, i, h)),
                  pl.BlockSpec((1, seq, LANES), lambda b, h, i: (b, 0, h)),
                  pl.BlockSpec((1, seq, 2 * DA_V_DIM), lambda b, h, i: (b, 0, h)),
                  pl.BlockSpec((1, DA_V_DIM), lambda b, h, i: (0, 0))],
        out_specs=pl.BlockSpec((1, tq, LANES), lambda b, h, i: (b, i, h)),
        out_shape=jax.ShapeDtypeStruct((bsz, seq, DA_WIDTH), BF16),
        scratch_shapes=[pltpu.VMEM((2 * tq, LANES), BF16),
                        pltpu.VMEM((2 * tq, LANES), F32),
                        pltpu.VMEM((2 * tq, 2 * DA_V_DIM), F32)],
        compiler_params=pltpu.CompilerParams(vmem_limit_bytes=VMEM_LIMIT),
        name="attn",
    )(lq1.reshape(1, -1), lk1.reshape(1, -1), lq2.reshape(1, -1), lk2.reshape(1, -1),
      q, k, v, head_g.reshape(1, -1))


def _merge_kernel(x_ref, mod_ref, g1n_ref, ya_ref, yb_ref, wg_ref, wa_ref, wb_ref, wo_ref,
                  g2n_ref, wq_ref, keys_ref, x1_ref, h2_ref, sct_ref):
    x = x_ref[...]
    sh1 = mod_ref[0, 0:1, :]
    sc1 = mod_ref[0, 1:2, :]
    g1 = mod_ref[0, 2:3, :]
    sh2 = mod_ref[0, 3:4, :]
    sc2 = mod_ref[0, 4:5, :]
    d = x.shape[1]
    h = (_rms(x) * g1n_ref[...] * (1.0 + sc1) + sh1).astype(BF16)
    ga = jax.nn.sigmoid(jnp.dot(h, wg_ref[:, 0:d], preferred_element_type=F32))
    a = jnp.dot(ya_ref[...], wa_ref[...], preferred_element_type=F32)
    merged = ga * a
    gb = jax.nn.sigmoid(jnp.dot(h, wg_ref[:, d:2 * d], preferred_element_type=F32))
    b = jnp.dot(yb_ref[...], wb_ref[...], preferred_element_type=F32)
    merged = (merged + gb * b).astype(BF16)
    x1 = x + g1 * jnp.dot(merged, wo_ref[...], preferred_element_type=F32)
    x1_ref[...] = x1
    h2 = _rms(x1) * g2n_ref[...] * (1.0 + sc2) + sh2
    h2_ref[...] = h2
    h2b = h2.astype(BF16)
    for hp in range(2 * PEER_HEADS):
        cs = slice(hp * PEER_HALF, (hp + 1) * PEER_HALF)
        qp = jnp.dot(h2b, wq_ref[:, cs], preferred_element_type=F32).astype(BF16)
        sct_ref[hp] = lax.dot_general(keys_ref[hp], qp, (((1,), (1,)), ((), ())),
                                      preferred_element_type=F32)


def _merge(xf, mod6, norm1_g, ya, yb, wg, wa, wb, wo, norm2_g, wq, keys, seq):
    n, d = xf.shape
    tm = min(MERGE_TM, seq)
    tiles_per_batch = seq // tm
    row = lambda i: (i, 0)
    const2 = lambda i: (0, 0)
    return pl.pallas_call(
        _merge_kernel,
        grid=(n // tm,),
        in_specs=[pl.BlockSpec((tm, d), row),
                  pl.BlockSpec((1, 6, d), lambda i: (i // tiles_per_batch, 0, 0)),
                  pl.BlockSpec((1, d), const2),
                  pl.BlockSpec((tm, DA_WIDTH), row),
                  pl.BlockSpec((tm, SG_WIDTH), row),
                  pl.BlockSpec(wg.shape, const2),
                  pl.BlockSpec(wa.shape, const2),
                  pl.BlockSpec(wb.shape, const2),
                  pl.BlockSpec(wo.shape, const2),
                  pl.BlockSpec((1, d), const2),
                  pl.BlockSpec(wq.shape, const2),
                  pl.BlockSpec(keys.shape, lambda i: (0, 0, 0))],
        out_specs=[pl.BlockSpec((tm, d), row),
                   pl.BlockSpec((tm, d), row),
                   pl.BlockSpec((2 * PEER_HEADS, PEER_KEYS, tm), lambda i: (0, 0, i))],
        out_shape=[jax.ShapeDtypeStruct((n, d), F32),
                   jax.ShapeDtypeStruct((n, d), F32),
                   jax.ShapeDtypeStruct((2 * PEER_HEADS, PEER_KEYS, n), F32)],
        compiler_params=pltpu.CompilerParams(vmem_limit_bytes=VMEM_LIMIT),
        name="merge",
    )(xf, mod6, norm1_g.reshape(1, d), ya, yb, wg, wa, wb, wo, norm2_g.reshape(1, d), wq, keys)


_CAND = [(i, j) for i in range(PEER_TOPK) for j in range(PEER_TOPK) if (i + 1) * (j + 1) <= PEER_TOPK]


def _topk_kernel(s_ref, e_ref, g_ref, w_ref, v_ref, i_ref):
    neg = jnp.float32(-jnp.inf)
    kio = lax.broadcasted_iota(jnp.int32, w_ref.shape, 0)

    for half in range(2):
        w_ref[...] = s_ref[0, half]

        def body(it, carry):
            s = w_ref[...]
            m = jnp.max(s, axis=0)
            idx = jnp.min(jnp.where(s == m[None], kio, PEER_KEYS), axis=0)
            w_ref[...] = jnp.where(kio == idx[None], neg, s)
            v_ref[half, it] = m
            i_ref[half, it] = idx
            return carry

        lax.fori_loop(0, PEER_TOPK, body, 0)

    cand = [v_ref[0, i] + v_ref[1, j] for (i, j) in _CAND]
    cid = [i_ref[0, i] * PEER_KEYS + i_ref[1, j] for (i, j) in _CAND]
    flat = [i * PEER_TOPK + j for (i, j) in _CAND]
    big = PEER_TOPK * PEER_TOPK
    tops = []
    for it in range(PEER_TOPK):
        m = cand[0]
        for c in cand[1:]:
            m = jnp.maximum(m, c)
        pos = jnp.full(m.shape, big, jnp.int32)
        for c, f in zip(cand, flat):
            pos = jnp.minimum(pos, jnp.where(c == m, f, big))
        e = jnp.zeros(m.shape, jnp.int32)
        for n, f in enumerate(flat):
            sel = pos == f
            e = jnp.where(sel, cid[n], e)
            cand[n] = jnp.where(sel, neg, cand[n])
        tops.append(m)
        e_ref[0, it] = e
    ex = [jnp.exp(t - tops[0]) for t in tops]
    tot = ex[0]
    for t in ex[1:]:
        tot = tot + t
    for it in range(PEER_TOPK):
        g_ref[0, it] = ex[it] / tot


def _topk(sct5):
    heads, _, keys, ng, _ = sct5.shape
    tg = min(TOPK_TG, ng)
    blk = (1, PEER_TOPK, tg, LANES)
    return pl.pallas_call(
        _topk_kernel,
        grid=(heads, ng // tg),
        in_specs=[pl.BlockSpec((1, 2, keys, tg, LANES), lambda h, t: (h, 0, 0, t, 0))],
        out_specs=[pl.BlockSpec(blk, lambda h, t: (h, 0, t, 0)),
                   pl.BlockSpec(blk, lambda h, t: (h, 0, t, 0))],
        out_shape=[jax.ShapeDtypeStruct((heads, PEER_TOPK, ng, LANES), jnp.int32),
                   jax.ShapeDtypeStruct((heads, PEER_TOPK, ng, LANES), F32)],
        scratch_shapes=[pltpu.VMEM((keys, tg, LANES), F32),
                        pltpu.VMEM((2, PEER_TOPK, tg, LANES), F32),
                        pltpu.VMEM((2, PEER_TOPK, tg, LANES), jnp.int32)],
        compiler_params=pltpu.CompilerParams(vmem_limit_bytes=VMEM_LIMIT),
        name="topk",
    )(sct5)


_NSLOT = 3
_BLOCK_PAIRS = PEER_TB * PEER_PAIRS


def _peer_kernel(idx01_ref, idxn_ref, tab_ref, gate_ref, h2_ref, x1_ref, g2_ref, fg_ref, o_ref,
                 buf_ref, sem_ref):
    i = pl.program_id(0)
    nsteps = pl.num_programs(0)
    slot = i % _NSLOT

    def slot_rows(s):
        return pl.ds(pl.multiple_of(s * SLAB, SLAB), SLAB)

    def slab_copy(e, p, dst_slot):
        return pltpu.make_async_copy(tab_ref.at[e], buf_ref.at[slot_rows(dst_slot), p],
                                     sem_ref.at[dst_slot])

    def issue(idx_ref, row0, t, dst_slot):
        for j in range(PEER_PAIRS):
            slab_copy(idx_ref[row0 + t, j], t * PEER_PAIRS + j, dst_slot).start(priority=j % 2)

    def wait_block(s):
        blk = buf_ref.at[slot_rows(s)]
        pltpu.make_async_copy(blk, blk, sem_ref.at[s]).wait()

    @pl.when(i == 0)
    def _():
        for t in range(PEER_TB):
            issue(idx01_ref, 0, t, 0)
        for t in range(PEER_TB):
            issue(idx01_ref, PEER_TB, t, 1)

    wait_block(slot)
    nslot = (i + 2) % _NSLOT
    base = slot * SLAB

    a_cols = []
    for t in range(PEER_TB):
        issue(idxn_ref, 0, t, nslot)
        ps = slice(t * PEER_PAIRS, (t + 1) * PEER_PAIRS)
        acc = buf_ref[base, ps, :] * h2_ref[t, 0:1, :]
        for s in range(1, SUBLANES):
            acc = acc + buf_ref[base + s, ps, :] * h2_ref[t, s:s + 1, :]
        a_cols.append(jnp.sum(acc, axis=1, keepdims=True))
    a = jnp.concatenate(a_cols, axis=1)
    w = gate_ref[0] * jax.nn.gelu(a)
    g2 = g2_ref[0]
    fg = fg_ref[...]
    for t in range(PEER_TB):
        ps = slice(t * PEER_PAIRS, (t + 1) * PEER_PAIRS)
        wb = jnp.broadcast_to(w[:, t:t + 1], (PEER_PAIRS, LANES))
        rows = [jnp.sum(buf_ref[base + SUBLANES + s, ps, :] * wb, axis=0, keepdims=True)
                for s in range(SUBLANES)]
        y = jnp.concatenate(rows, axis=0)
        x2 = x1_ref[t] + g2 * y
        ms = jnp.sum(jnp.sum(x2 * x2, axis=1, keepdims=True), axis=0, keepdims=True)
        o_ref[t] = x2 * lax.rsqrt(ms / (SUBLANES * LANES) + EPS) * fg

    @pl.when(i == nsteps - 1)
    def _():
        wait_block((i + 1) % _NSLOT)
        wait_block(nslot)


def _peer(eidx, gate_t, tab, h2r, x1r, g2r, fgr, seq):
    n = eidx.shape[0]
    tb = PEER_TB
    nsteps = n // tb
    assert nsteps >= 2
    steps_per_batch = seq // tb
    tok = lambda i: (i, 0, 0)
    return pl.pallas_call(
        _peer_kernel,
        grid=(nsteps,),
        in_specs=[pl.BlockSpec((2 * tb, PEER_PAIRS), lambda i: (0, 0), memory_space=pltpu.SMEM),
                  pl.BlockSpec((tb, PEER_PAIRS), lambda i: (jnp.minimum(i + 2, nsteps - 1), 0),
                               memory_space=pltpu.SMEM),
                  pl.BlockSpec(memory_space=pl.ANY),
                  pl.BlockSpec((1, PEER_PAIRS, tb), tok),
                  pl.BlockSpec((tb, SUBLANES, LANES), tok),
                  pl.BlockSpec((tb, SUBLANES, LANES), tok),
                  pl.BlockSpec((1, SUBLANES, LANES), lambda i: (i // steps_per_batch, 0, 0)),
                  pl.BlockSpec((SUBLANES, LANES), lambda i: (0, 0))],
        out_specs=pl.BlockSpec((tb, SUBLANES, LANES), tok),
        out_shape=jax.ShapeDtypeStruct((n, SUBLANES, LANES), F32),
        scratch_shapes=[pltpu.VMEM((_NSLOT * SLAB, _BLOCK_PAIRS, LANES), F32),
                        pltpu.SemaphoreType.DMA((_NSLOT,))],
        compiler_params=pltpu.CompilerParams(dimension_semantics=("arbitrary",),
                                             vmem_limit_bytes=VMEM_LIMIT),
        name="peer",
    )(eidx, eidx, tab, gate_t, h2r, x1r, g2r, fgr)


def kernel(x, c, w_ada, b_ada, norm1_g, w_in, da_lambda_q1, da_lambda_k1, da_lambda_q2, da_lambda_k2, da_head_g, sg_ln_g, sg_ln_b, sg_w, sg_b, w_branch_a, w_branch_b, w_out, norm2_g, peer_w_query, peer_sub_keys, peer_down, peer_up, final_g):
    bsz, seq, d = x.shape
    n = bsz * seq
    assert w_ada.shape[0] == 1 and d == SUBLANES * LANES
    xf = x.reshape(n, d)

    mod6 = _ada(c, w_ada[0], b_ada[0]).reshape(bsz, 6, d)

    nproj = 2 * DA_QK_COLS + DA_WIDTH + 2 * SG_WIDTH
    w1 = w_in[0][:, :nproj].astype(BF16)
    wg = w_in[0][:, nproj:].astype(BF16)
    q, k, v, yb = _proj(xf, mod6, norm1_g[0], w1, sg_ln_g[0], sg_ln_b[0], sg_w[0], sg_b[0], seq)

    ya = _attn(q.reshape(bsz, seq, -1), k.reshape(bsz, seq, -1), v.reshape(bsz, seq, -1),
               da_lambda_q1[0], da_lambda_k1[0], da_lambda_q2[0], da_lambda_k2[0], da_head_g[0])

    keys = peer_sub_keys[0].reshape(2 * PEER_HEADS, PEER_KEYS, PEER_HALF).astype(BF16)
    x1, h2, sct = _merge(xf, mod6, norm1_g[0], ya.reshape(n, -1), yb, wg,
                         w_branch_a[0].astype(BF16), w_branch_b[0].astype(BF16),
                         w_out[0].astype(BF16), norm2_g[0], peer_w_query[0].astype(BF16), keys, seq)

    sct5 = sct.reshape(PEER_HEADS, 2, PEER_KEYS, n // LANES, LANES)
    eidx_t, gate = _topk(sct5)
    eidx = eidx_t.reshape(PEER_PAIRS, n).T
    gate_t = gate.reshape(PEER_PAIRS, n // PEER_TB, PEER_TB).transpose(1, 0, 2)

    tab = jnp.concatenate([peer_down[0], peer_up[0]], axis=1).reshape(-1, SLAB, LANES)
    out = _peer(eidx, gate_t, tab,
                h2.reshape(n, SUBLANES, LANES), x1.reshape(n, SUBLANES, LANES),
                mod6[:, 5].reshape(bsz, SUBLANES, LANES), final_g.reshape(SUBLANES, LANES), seq)
    return out.reshape(bsz, seq, d)
```

```python
import functools

import jax
import jax.numpy as jnp
from jax import lax
from jax.experimental import pallas as pl
from jax.experimental.pallas import tpu as pltpu

F32 = jnp.float32
BF16 = jnp.bfloat16

EPS = 1e-6
CHUNK = 64
DA_HEADS = 4
DA_HEAD_DIM = 64
DA_V_DIM = 128
DA_QK_COLS = 512
DA_WIDTH = 512
SG_GROUPS = 4
SG_BLOCK = 128
SG_WIDTH = 512
PEER_HEADS = 8
PEER_KEYS = 128
PEER_HALF = 128
PEER_TOPK = 16
LAM_INIT = 0.8 - 0.6 * 1.0

LANES = 128
SUBLANES = 8

PROJ_TM = 512
MERGE_TM = 256
ATTN_TQ = 512
ATTN_RB = 256
TOPK_TG = 8
PEER_TB = 8
PEER_PAIRS = PEER_HEADS * PEER_TOPK
SLAB = 2 * SUBLANES
VMEM_LIMIT = 56 * 1024 * 1024


def _rms(x):
    return x * lax.rsqrt(jnp.mean(x * x, axis=-1, keepdims=True) + EPS)


def _ada_kernel(c_ref, w_ref, b_ref, o_ref):
    c = c_ref[...]
    s = (c * jax.nn.sigmoid(c)).astype(BF16)
    o_ref[...] = jnp.dot(s, w_ref[...].astype(BF16), preferred_element_type=F32) + b_ref[...]


def _ada(c, w, b):
    bsz, d = c.shape
    e = w.shape[1]
    tn = 1536
    return pl.pallas_call(
        _ada_kernel,
        grid=(e // tn,),
        in_specs=[pl.BlockSpec((bsz, d), lambda j: (0, 0)),
                  pl.BlockSpec((d, tn), lambda j: (0, j)),
                  pl.BlockSpec((1, tn), lambda j: (0, j))],
        out_specs=pl.BlockSpec((bsz, tn), lambda j: (0, j)),
        out_shape=jax.ShapeDtypeStruct((bsz, e), F32),
        name="ada",
    )(c, w, b.reshape(1, e))


def _proj_kernel(x_ref, mod_ref, g_ref, w_ref, lng_ref, lnb_ref, sgw_ref, sgb_ref,
                 q_ref, k_ref, v_ref, yb_ref):
    x = x_ref[...]
    sh1 = mod_ref[0, 0:1, :]
    sc1 = mod_ref[0, 1:2, :]
    h = (_rms(x) * g_ref[...] * (1.0 + sc1) + sh1).astype(BF16)

    def seg(a, b):
        return jnp.dot(h, w_ref[:, a:b], preferred_element_type=F32)

    q_ref[...] = (seg(0, 512) * (DA_HEAD_DIM ** -0.5)).astype(BF16)
    k_ref[...] = seg(512, 1024).astype(BF16)
    vv = seg(1024, 1536).astype(BF16)
    ones = jnp.ones((vv.shape[0], DA_V_DIM), BF16)
    for hh in range(DA_HEADS):
        v_ref[:, 2 * hh * DA_V_DIM:(2 * hh + 1) * DA_V_DIM] = vv[:, hh * DA_V_DIM:(hh + 1) * DA_V_DIM]
        v_ref[:, (2 * hh + 1) * DA_V_DIM:(2 * hh + 2) * DA_V_DIM] = ones
    u = jax.nn.gelu(seg(1536, 2048))
    sv = jax.nn.gelu(seg(2048, 2560))
    mu = jnp.mean(sv, axis=-1, keepdims=True)
    svc = sv - mu
    svn = svc * lax.rsqrt(jnp.mean(svc * svc, axis=-1, keepdims=True) + EPS)
    svn = (svn * lng_ref[...] + lnb_ref[...]).astype(BF16)
    pr = lax.broadcasted_iota(jnp.int32, (SG_BLOCK, SG_BLOCK), 0) // CHUNK
    pc = lax.broadcasted_iota(jnp.int32, (SG_BLOCK, SG_BLOCK), 1) // CHUNK
    keep = pr >= pc
    tm = x.shape[0]
    for g in range(SG_GROUPS):
        wm = jnp.where(keep, sgw_ref[g], 0.0).astype(BF16)
        bias = sgb_ref[g]
        cs = slice(g * SG_BLOCK, (g + 1) * SG_BLOCK)
        for r in range(tm // SG_BLOCK):
            rs = slice(r * SG_BLOCK, (r + 1) * SG_BLOCK)
            mixed = jnp.dot(wm, svn[rs, cs], preferred_element_type=F32) + bias
            yb_ref[rs, cs] = (u[rs, cs] * mixed).astype(BF16)


def _proj(xf, mod6, norm_g, w1, ln_g, ln_b, sg_w, sg_b, seq):
    n, d = xf.shape
    tm = min(PROJ_TM, seq)
    tiles_per_batch = seq // tm
    row = lambda i: (i, 0)
    const2 = lambda i: (0, 0)
    const3 = lambda i: (0, 0, 0)
    out = jax.ShapeDtypeStruct((n, 512), BF16)
    return pl.pallas_call(
        _proj_kernel,
        grid=(n // tm,),
        in_specs=[pl.BlockSpec((tm, d), row),
                  pl.BlockSpec((1, 6, d), lambda i: (i // tiles_per_batch, 0, 0)),
                  pl.BlockSpec((1, d), const2),
                  pl.BlockSpec(w1.shape, const2),
                  pl.BlockSpec((1, SG_WIDTH), const2),
                  pl.BlockSpec((1, SG_WIDTH), const2),
                  pl.BlockSpec((SG_GROUPS, SG_BLOCK, SG_BLOCK), const3),
                  pl.BlockSpec((SG_GROUPS, SG_BLOCK, 1), const3)],
        out_specs=[pl.BlockSpec((tm, 512), row), pl.BlockSpec((tm, 512), row),
                   pl.BlockSpec((tm, 2 * DA_WIDTH), row), pl.BlockSpec((tm, 512), row)],
        out_shape=[out, out, jax.ShapeDtypeStruct((n, 2 * DA_WIDTH), BF16), out],
        compiler_params=pltpu.CompilerParams(vmem_limit_bytes=VMEM_LIMIT),
        name="proj",
    )(xf, mod6, norm_g.reshape(1, d), w1, ln_g.reshape(1, -1), ln_b.reshape(1, -1),
      sg_w, sg_b[..., None])


def _attn_kernel(lq1_ref, lk1_ref, lq2_ref, lk2_ref, q_ref, k_ref, v_ref, hg_ref, o_ref,
                 qs_ref, m_ref, acc_ref):
    tq = q_ref.shape[1]
    qi = pl.program_id(2)
    q = q_ref[0]
    lane = lax.broadcasted_iota(jnp.int32, q.shape, 1)
    zero = jnp.zeros_like(q)
    qs_ref[0:tq, :] = jnp.where(lane < DA_HEAD_DIM, q, zero)
    qs_ref[tq:2 * tq, :] = jnp.where(lane >= DA_HEAD_DIM, q, zero)
    m_ref[...] = jnp.full(m_ref.shape, -1e30, F32)
    acc_ref[...] = jnp.zeros(acc_ref.shape, F32)
    rb_rows = min(ATTN_RB, tq)
    nlc = tq // LANES

    def step(j, masked):
        off = pl.multiple_of(j * tq, tq)
        kt = k_ref[0, pl.ds(off, tq), :]
        vt = v_ref[0, pl.ds(off, tq), :]
        for rb in range(2 * tq // rb_rows):
            rs = slice(rb * rb_rows, (rb + 1) * rb_rows)
            s = lax.dot_general(qs_ref[rs, :], kt, (((1,), (1,)), ((), ())),
                                preferred_element_type=F32)
            if masked:
                qpos = (rb * rb_rows) % tq + lax.broadcasted_iota(jnp.int32, s.shape, 0)
                kpos = lax.broadcasted_iota(jnp.int32, s.shape, 1)
                s = jnp.where(kpos // CHUNK <= qpos // CHUNK, s, -1e30)
            m_old = m_ref[rs, :]
            m_new = jnp.maximum(m_old, jnp.max(s, axis=-1, keepdims=True))
            alpha = jnp.exp(m_old - m_new)
            p = jnp.exp(s - jnp.concatenate([m_new] * nlc, axis=1)).astype(BF16)
            pv = jnp.dot(p, vt, preferred_element_type=F32)
            acc_ref[rs, :] = acc_ref[rs, :] * jnp.concatenate([alpha, alpha], axis=1) + pv
            m_ref[rs, :] = m_new

    def body(j, carry):
        step(j, False)
        return carry

    lax.fori_loop(0, qi, body, 0)
    step(qi, True)

    lam = (jnp.exp(jnp.sum(lq1_ref[...] * lk1_ref[...], axis=-1, keepdims=True))
           - jnp.exp(jnp.sum(lq2_ref[...] * lk2_ref[...], axis=-1, keepdims=True)) + LAM_INIT)
    o1 = acc_ref[0:tq, 0:DA_V_DIM] / acc_ref[0:tq, DA_V_DIM:2 * DA_V_DIM]
    o2 = acc_ref[tq:2 * tq, 0:DA_V_DIM] / acc_ref[tq:2 * tq, DA_V_DIM:2 * DA_V_DIM]
    o = o1 - lam * o2
    o = _rms(o) * hg_ref[...] * (1.0 - LAM_INIT)
    o_ref[0] = o.astype(BF16)


def _attn(q, k, v, lq1, lk1, lq2, lk2, head_g):
    bsz, seq, _ = q.shape
    tq = min(ATTN_TQ, seq)
    lam_spec = pl.BlockSpec((1, DA_HEAD_DIM), lambda b, h, i: (0, 0))
    return pl.pallas_call(
        _attn_kernel,
        grid=(bsz, DA_HEADS, seq // tq),
        in_specs=[lam_spec, lam_spec, lam_spec, lam_spec,
                  pl.BlockSpec((1, tq, LANES), lambda b, h, i: (b, i, h)),
                  pl.BlockSpec((1, seq, LANES), lambda b, h, i: (b, 0, h)),
                  pl.BlockSpec((1, seq, 2 * DA_V_DIM), lambda b, h, i: (b, 0, h)),
                  pl.BlockSpec((1, DA_V_DIM), lambda b, h, i: (0, 0))],
        out_specs=pl.BlockSpec((1, tq, LANES), lambda b, h, i: (b, i, h)),
        out_shape=jax.ShapeDtypeStruct((bsz, seq, DA_WIDTH), BF16),
        scratch_shapes=[pltpu.VMEM((2 * tq, LANES), BF16),
                        pltpu.VMEM((2 * tq, LANES), F32),
                        pltpu.VMEM((2 * tq, 2 * DA_V_DIM), F32)],
        compiler_params=pltpu.CompilerParams(vmem_limit_bytes=VMEM_LIMIT),
        name="attn",
    )(lq1.reshape(1, -1), lk1.reshape(1, -1), lq2.reshape(1, -1), lk2.reshape(1, -1),
      q, k, v, head_g.reshape(1, -1))


def _merge_kernel(x_ref, mod_ref, g1n_ref, ya_ref, yb_ref, wg_ref, wa_ref, wb_ref, wo_ref,
                  g2n_ref, wq_ref, keys_ref, x1_ref, h2_ref, sct_ref):
    x = x_ref[...]
    sh1 = mod_ref[0, 0:1, :]
    sc1 = mod_ref[0, 1:2, :]
    g1 = mod_ref[0, 2:3, :]
    sh2 = mod_ref[0, 3:4, :]
    sc2 = mod_ref[0, 4:5, :]
    d = x.shape[1]
    h = (_rms(x) * g1n_ref[...] * (1.0 + sc1) + sh1).astype(BF16)
    ga = jax.nn.sigmoid(jnp.dot(h, wg_ref[:, 0:d], preferred_element_type=F32))
    a = jnp.dot(ya_ref[...], wa_ref[...], preferred_element_type=F32)
    merged = ga * a
    gb = jax.nn.sigmoid(jnp.dot(h, wg_ref[:, d:2 * d], preferred_element_type=F32))
    b = jnp.dot(yb_ref[...], wb_ref[...], preferred_element_type=F32)
    merged = (merged + gb * b).astype(BF16)
    x1 = x + g1 * jnp.dot(merged, wo_ref[...], preferred_element_type=F32)
    x1_ref[...] = x1
    h2 = _rms(x1) * g2n_ref[...] * (1.0 + sc2) + sh2
    h2_ref[...] = h2
    h2b = h2.astype(BF16)
    for hp in range(2 * PEER_HEADS):
        cs = slice(hp * PEER_HALF, (hp + 1) * PEER_HALF)
        qp = jnp.dot(h2b, wq_ref[:, cs], preferred_element_type=F32).astype(BF16)
        sct_ref[hp] = lax.dot_general(keys_ref[hp], qp, (((1,), (1,)), ((), ())),
                                      preferred_element_type=F32)


def _merge(xf, mod6, norm1_g, ya, yb, wg, wa, wb, wo, norm2_g, wq, keys, seq):
    n, d = xf.shape
    tm = min(MERGE_TM, seq)
    tiles_per_batch = seq // tm
    row = lambda i: (i, 0)
    const2 = lambda i: (0, 0)
    return pl.pallas_call(
        _merge_kernel,
        grid=(n // tm,),
        in_specs=[pl.BlockSpec((tm, d), row),
                  pl.BlockSpec((1, 6, d), lambda i: (i // tiles_per_batch, 0, 0)),
                  pl.BlockSpec((1, d), const2),
                  pl.BlockSpec((tm, DA_WIDTH), row),
                  pl.BlockSpec((tm, SG_WIDTH), row),
                  pl.BlockSpec(wg.shape, const2),
                  pl.BlockSpec(wa.shape, const2),
                  pl.BlockSpec(wb.shape, const2),
                  pl.BlockSpec(wo.shape, const2),
                  pl.BlockSpec((1, d), const2),
                  pl.BlockSpec(wq.shape, const2),
                  pl.BlockSpec(keys.shape, lambda i: (0, 0, 0))],
        out_specs=[pl.BlockSpec((tm, d), row),
                   pl.BlockSpec((tm, d), row),
                   pl.BlockSpec((2 * PEER_HEADS, PEER_KEYS, tm), lambda i: (0, 0, i))],
        out_shape=[jax.ShapeDtypeStruct((n, d), F32),
                   jax.ShapeDtypeStruct((n, d), F32),
                   jax.ShapeDtypeStruct((2 * PEER_HEADS, PEER_KEYS, n), F32)],
        compiler_params=pltpu.CompilerParams(vmem_limit_bytes=VMEM_LIMIT),
        name="merge",
    )(xf, mod6, norm1_g.reshape(1, d), ya, yb, wg, wa, wb, wo, norm2_g.reshape(1, d), wq, keys)


_CAND = [(i, j) for i in range(PEER_TOPK) for j in range(PEER_TOPK) if (i + 1) * (j + 1) <= PEER_TOPK]
_TOPK_CHUNK = 16


def _best_of(items):
    while len(items) > 1:
        nxt = []
        for a in range(0, len(items) - 1, 2):
            left, right = items[a], items[a + 1]
            keep = left[0] >= right[0]
            nxt.append(tuple(jnp.where(keep, x, y) for x, y in zip(left, right)))
        if len(items) % 2:
            nxt.append(items[-1])
        items = nxt
    return items[0]


def _topk_kernel(s_ref, e_ref, g_ref, w_ref, v_ref, i_ref):
    neg = jnp.float32(-jnp.inf)
    nkeys = w_ref.shape[1]
    w_ref[...] = s_ref[0]

    def body(it, prevs):
        outs = []
        for half in range(2):
            prev = prevs[half]
            partials = []
            for c0 in range(0, nkeys, _TOPK_CHUNK):
                items = []
                for k in range(c0, c0 + _TOPK_CHUNK):
                    s = jnp.where(prev == float(k), neg, w_ref[half, k])
                    w_ref[half, k] = s
                    items.append((s, float(k)))
                partials.append(_best_of(items))
            m, idx = _best_of(partials)
            v_ref[half, it] = m
            i_ref[half, it] = idx.astype(jnp.int32)
            outs.append(idx)
        return tuple(outs)

    none = jnp.full(w_ref.shape[2:], -1.0, F32)
    lax.fori_loop(0, PEER_TOPK, body, (none, none))

    cand = [v_ref[0, i] + v_ref[1, j] for (i, j) in _CAND]
    cid = [i_ref[0, i] * PEER_KEYS + i_ref[1, j] for (i, j) in _CAND]
    flat = [float(i * PEER_TOPK + j) for (i, j) in _CAND]
    tops = []
    pos = None
    for it in range(PEER_TOPK):
        if pos is not None:
            cand = [jnp.where(pos == f, neg, c) for c, f in zip(cand, flat)]
        m, pos, e = _best_of(list(zip(cand, flat, cid)))
        tops.append(m)
        e_ref[0, it] = e
    ex = [jnp.exp(t - tops[0]) for t in tops]
    tot = ex[0]
    for t in ex[1:]:
        tot = tot + t
    for it in range(PEER_TOPK):
        g_ref[0, it] = ex[it] / tot


def _topk(sct5):
    heads, _, keys, ng, _ = sct5.shape
    tg = min(TOPK_TG, ng)
    blk = (1, PEER_TOPK, tg, LANES)
    return pl.pallas_call(
        _topk_kernel,
        grid=(heads, ng // tg),
        in_specs=[pl.BlockSpec((1, 2, keys, tg, LANES), lambda h, t: (h, 0, 0, t, 0))],
        out_specs=[pl.BlockSpec(blk, lambda h, t: (h, 0, t, 0)),
                   pl.BlockSpec(blk, lambda h, t: (h, 0, t, 0))],
        out_shape=[jax.ShapeDtypeStruct((heads, PEER_TOPK, ng, LANES), jnp.int32),
                   jax.ShapeDtypeStruct((heads, PEER_TOPK, ng, LANES), F32)],
        scratch_shapes=[pltpu.VMEM((2, keys, tg, LANES), F32),
                        pltpu.VMEM((2, PEER_TOPK, tg, LANES), F32),
                        pltpu.VMEM((2, PEER_TOPK, tg, LANES), jnp.int32)],
        compiler_params=pltpu.CompilerParams(vmem_limit_bytes=VMEM_LIMIT),
        name="topk",
    )(sct5)


_NSLOT = 3
_BLOCK_PAIRS = PEER_TB * PEER_PAIRS


def _peer_kernel(idx01_ref, idxn_ref, tab_ref, gate_ref, h2_ref, x1_ref, g2_ref, fg_ref, o_ref,
                 buf_ref, wb_ref, sem_ref):
    i = pl.program_id(0)
    nsteps = pl.num_programs(0)
    slot = i % _NSLOT

    def slab_copy(e, p, dst_slot):
        return pltpu.make_async_copy(tab_ref.at[e], buf_ref.at[dst_slot, p], sem_ref.at[dst_slot])

    def issue(idx_ref, row0, t, dst_slot):
        for j in range(PEER_PAIRS):
            slab_copy(idx_ref[row0 + t, j], t * PEER_PAIRS + j, dst_slot).start(priority=j % 2)

    def wait_block(s):
        pltpu.make_async_copy(buf_ref.at[s], buf_ref.at[s], sem_ref.at[s]).wait()

    @pl.when(i == 0)
    def _():
        for t in range(PEER_TB):
            issue(idx01_ref, 0, t, 0)
        for t in range(PEER_TB):
            issue(idx01_ref, PEER_TB, t, 1)

    wait_block(slot)
    nslot = (i + 2) % _NSLOT
    lane = lax.broadcasted_iota(jnp.int32, (SUBLANES, LANES), 1)

    a_rows = []
    for t in range(PEER_TB):
        issue(idxn_ref, 0, t, nslot)
        hv = h2_ref[t]
        part = jnp.zeros((SUBLANES, LANES), F32)
        for j in range(PEER_PAIRS):
            prod = buf_ref[slot, t * PEER_PAIRS + j, 0:SUBLANES, :] * hv
            part = jnp.where(lane == j, jnp.sum(prod, axis=1, keepdims=True), part)
        a_rows.append(jnp.sum(part, axis=0, keepdims=True))
    a = jnp.concatenate(a_rows, axis=0)
    w = gate_ref[...] * jax.nn.gelu(a)
    for j in range(PEER_PAIRS):
        wb_ref[j] = jnp.broadcast_to(w[:, j:j + 1], (PEER_TB, LANES))
    g2 = g2_ref[0]
    fg = fg_ref[...]
    for t in range(PEER_TB):
        y = jnp.zeros((SUBLANES, LANES), F32)
        for j in range(PEER_PAIRS):
            up = buf_ref[slot, t * PEER_PAIRS + j, SUBLANES:SLAB, :]
            y = y + up * jnp.broadcast_to(wb_ref[j, t:t + 1, :], (SUBLANES, LANES))
        x2 = x1_ref[t] + g2 * y
        ms = jnp.sum(jnp.sum(x2 * x2, axis=1, keepdims=True), axis=0, keepdims=True)
        o_ref[t] = x2 * lax.rsqrt(ms / (SUBLANES * LANES) + EPS) * fg

    @pl.when(i == nsteps - 1)
    def _():
        wait_block((i + 1) % _NSLOT)
        wait_block(nslot)


def _peer(eidx, gate, tab, h2r, x1r, g2r, fgr, seq):
    n = eidx.shape[0]
    tb = PEER_TB
    nsteps = n // tb
    assert nsteps >= 2 and tb == SUBLANES
    steps_per_batch = seq // tb
    tok = lambda i: (i, 0, 0)
    return pl.pallas_call(
        _peer_kernel,
        grid=(nsteps,),
        in_specs=[pl.BlockSpec((2 * tb, PEER_PAIRS), lambda i: (0, 0), memory_space=pltpu.SMEM),
                  pl.BlockSpec((tb, PEER_PAIRS), lambda i: (jnp.minimum(i + 2, nsteps - 1), 0),
                               memory_space=pltpu.SMEM),
                  pl.BlockSpec(memory_space=pl.ANY),
                  pl.BlockSpec((tb, PEER_PAIRS), lambda i: (i, 0)),
                  pl.BlockSpec((tb, SUBLANES, LANES), tok),
                  pl.BlockSpec((tb, SUBLANES, LANES), tok),
                  pl.BlockSpec((1, SUBLANES, LANES), lambda i: (i // steps_per_batch, 0, 0)),
                  pl.BlockSpec((SUBLANES, LANES), lambda i: (0, 0))],
        out_specs=pl.BlockSpec((tb, SUBLANES, LANES), tok),
        out_shape=jax.ShapeDtypeStruct((n, SUBLANES, LANES), F32),
        scratch_shapes=[pltpu.VMEM((_NSLOT, _BLOCK_PAIRS, SLAB, LANES), F32),
                        pltpu.VMEM((PEER_PAIRS, tb, LANES), F32),
                        pltpu.SemaphoreType.DMA((_NSLOT,))],
        compiler_params=pltpu.CompilerParams(dimension_semantics=("arbitrary",),
                                             vmem_limit_bytes=VMEM_LIMIT),
        name="peer",
    )(eidx, eidx, tab, gate, h2r, x1r, g2r, fgr)


def kernel(x, c, w_ada, b_ada, norm1_g, w_in, da_lambda_q1, da_lambda_k1, da_lambda_q2, da_lambda_k2, da_head_g, sg_ln_g, sg_ln_b, sg_w, sg_b, w_branch_a, w_branch_b, w_out, norm2_g, peer_w_query, peer_sub_keys, peer_down, peer_up, final_g):
    bsz, seq, d = x.shape
    n = bsz * seq
    assert w_ada.shape[0] == 1 and d == SUBLANES * LANES
    xf = x.reshape(n, d)

    mod6 = _ada(c, w_ada[0], b_ada[0]).reshape(bsz, 6, d)

    nproj = 2 * DA_QK_COLS + DA_WIDTH + 2 * SG_WIDTH
    w1 = w_in[0][:, :nproj].astype(BF16)
    wg = w_in[0][:, nproj:].astype(BF16)
    q, k, v, yb = _proj(xf, mod6, norm1_g[0], w1, sg_ln_g[0], sg_ln_b[0], sg_w[0], sg_b[0], seq)

    ya = _attn(q.reshape(bsz, seq, -1), k.reshape(bsz, seq, -1), v.reshape(bsz, seq, -1),
               da_lambda_q1[0], da_lambda_k1[0], da_lambda_q2[0], da_lambda_k2[0], da_head_g[0])

    keys = peer_sub_keys[0].reshape(2 * PEER_HEADS, PEER_KEYS, PEER_HALF).astype(BF16)
    x1, h2, sct = _merge(xf, mod6, norm1_g[0], ya.reshape(n, -1), yb, wg,
                         w_branch_a[0].astype(BF16), w_branch_b[0].astype(BF16),
                         w_out[0].astype(BF16), norm2_g[0], peer_w_query[0].astype(BF16), keys, seq)

    sct5 = sct.reshape(PEER_HEADS, 2, PEER_KEYS, n // LANES, LANES)
    eidx_t, gate = _topk(sct5)
    eidx = eidx_t.reshape(PEER_PAIRS, n).T
    gate = gate.reshape(PEER_PAIRS, n).T

    tab = jnp.concatenate([peer_down[0], peer_up[0]], axis=1).reshape(-1, SLAB, LANES)
    out = _peer(eidx, gate, tab,
                h2.reshape(n, SUBLANES, LANES), x1.reshape(n, SUBLANES, LANES),
                mod6[:, 5].reshape(bsz, SUBLANES, LANES), final_g.reshape(SUBLANES, LANES), seq)
    return out.reshape(bsz, seq, d)
```

```python
import functools

import jax
import jax.numpy as jnp
from jax import lax
from jax.experimental import pallas as pl
from jax.experimental.pallas import tpu as pltpu

F32 = jnp.float32
BF16 = jnp.bfloat16

EPS = 1e-6
CHUNK = 64
DA_HEADS = 4
DA_HEAD_DIM = 64
DA_V_DIM = 128
DA_QK_COLS = 512
DA_WIDTH = 512
SG_GROUPS = 4
SG_BLOCK = 128
SG_WIDTH = 512
PEER_HEADS = 8
PEER_KEYS = 128
PEER_HALF = 128
PEER_TOPK = 16
LAM_INIT = 0.8 - 0.6 * 1.0

LANES = 128
SUBLANES = 8

PROJ_TM = 512
MERGE_TM = 256
ATTN_TQ = 512
ATTN_RB = 256
TOPK_TG = 8
PEER_TB = 8
PEER_PAIRS = PEER_HEADS * PEER_TOPK
SLAB = 2 * SUBLANES
VMEM_LIMIT = 56 * 1024 * 1024


def _rms(x):
    return x * lax.rsqrt(jnp.mean(x * x, axis=-1, keepdims=True) + EPS)


def _ada_kernel(c_ref, w_ref, b_ref, o_ref):
    c = c_ref[...]
    s = (c * jax.nn.sigmoid(c)).astype(BF16)
    o_ref[...] = jnp.dot(s, w_ref[...].astype(BF16), preferred_element_type=F32) + b_ref[...]


def _ada(c, w, b):
    bsz, d = c.shape
    e = w.shape[1]
    tn = 1536
    return pl.pallas_call(
        _ada_kernel,
        grid=(e // tn,),
        in_specs=[pl.BlockSpec((bsz, d), lambda j: (0, 0)),
                  pl.BlockSpec((d, tn), lambda j: (0, j)),
                  pl.BlockSpec((1, tn), lambda j: (0, j))],
        out_specs=pl.BlockSpec((bsz, tn), lambda j: (0, j)),
        out_shape=jax.ShapeDtypeStruct((bsz, e), F32),
        name="ada",
    )(c, w, b.reshape(1, e))


def _proj_kernel(x_ref, mod_ref, g_ref, w_ref, lng_ref, lnb_ref, sgw_ref, sgb_ref,
                 q_ref, k_ref, v_ref, yb_ref):
    x = x_ref[...]
    sh1 = mod_ref[0, 0:1, :]
    sc1 = mod_ref[0, 1:2, :]
    h = (_rms(x) * g_ref[...] * (1.0 + sc1) + sh1).astype(BF16)

    def seg(a, b):
        return jnp.dot(h, w_ref[:, a:b], preferred_element_type=F32)

    q_ref[...] = (seg(0, 512) * (DA_HEAD_DIM ** -0.5)).astype(BF16)
    k_ref[...] = seg(512, 1024).astype(BF16)
    vv = seg(1024, 1536).astype(BF16)
    ones = jnp.ones((vv.shape[0], DA_V_DIM), BF16)
    for hh in range(DA_HEADS):
        v_ref[:, 2 * hh * DA_V_DIM:(2 * hh + 1) * DA_V_DIM] = vv[:, hh * DA_V_DIM:(hh + 1) * DA_V_DIM]
        v_ref[:, (2 * hh + 1) * DA_V_DIM:(2 * hh + 2) * DA_V_DIM] = ones
    u = jax.nn.gelu(seg(1536, 2048))
    sv = jax.nn.gelu(seg(2048, 2560))
    mu = jnp.mean(sv, axis=-1, keepdims=True)
    svc = sv - mu
    svn = svc * lax.rsqrt(jnp.mean(svc * svc, axis=-1, keepdims=True) + EPS)
    svn = (svn * lng_ref[...] + lnb_ref[...]).astype(BF16)
    pr = lax.broadcasted_iota(jnp.int32, (SG_BLOCK, SG_BLOCK), 0) // CHUNK
    pc = lax.broadcasted_iota(jnp.int32, (SG_BLOCK, SG_BLOCK), 1) // CHUNK
    keep = pr >= pc
    tm = x.shape[0]
    for g in range(SG_GROUPS):
        wm = jnp.where(keep, sgw_ref[g], 0.0).astype(BF16)
        bias = sgb_ref[g]
        cs = slice(g * SG_BLOCK, (g + 1) * SG_BLOCK)
        for r in range(tm // SG_BLOCK):
            rs = slice(r * SG_BLOCK, (r + 1) * SG_BLOCK)
            mixed = jnp.dot(wm, svn[rs, cs], preferred_element_type=F32) + bias
            yb_ref[rs, cs] = (u[rs, cs] * mixed).astype(BF16)


def _proj(xf, mod6, norm_g, w1, ln_g, ln_b, sg_w, sg_b, seq):
    n, d = xf.shape
    tm = min(PROJ_TM, seq)
    tiles_per_batch = seq // tm
    row = lambda i: (i, 0)
    const2 = lambda i: (0, 0)
    const3 = lambda i: (0, 0, 0)
    out = jax.ShapeDtypeStruct((n, 512), BF16)
    return pl.pallas_call(
        _proj_kernel,
        grid=(n // tm,),
        in_specs=[pl.BlockSpec((tm, d), row),
                  pl.BlockSpec((1, 6, d), lambda i: (i // tiles_per_batch, 0, 0)),
                  pl.BlockSpec((1, d), const2),
                  pl.BlockSpec(w1.shape, const2),
                  pl.BlockSpec((1, SG_WIDTH), const2),
                  pl.BlockSpec((1, SG_WIDTH), const2),
                  pl.BlockSpec((SG_GROUPS, SG_BLOCK, SG_BLOCK), const3),
                  pl.BlockSpec((SG_GROUPS, SG_BLOCK, 1), const3)],
        out_specs=[pl.BlockSpec((tm, 512), row), pl.BlockSpec((tm, 512), row),
                   pl.BlockSpec((tm, 2 * DA_WIDTH), row), pl.BlockSpec((tm, 512), row)],
        out_shape=[out, out, jax.ShapeDtypeStruct((n, 2 * DA_WIDTH), BF16), out],
        compiler_params=pltpu.CompilerParams(vmem_limit_bytes=VMEM_LIMIT),
        name="proj",
    )(xf, mod6, norm_g.reshape(1, d), w1, ln_g.reshape(1, -1), ln_b.reshape(1, -1),
      sg_w, sg_b[..., None])


def _attn_kernel(lq1_ref, lk1_ref, lq2_ref, lk2_ref, q_ref, k_ref, v_ref, hg_ref, o_ref,
                 qs_ref, m_ref, acc_ref):
    tq = q_ref.shape[1]
    qi = pl.program_id(2)
    q = q_ref[0]
    lane = lax.broadcasted_iota(jnp.int32, q.shape, 1)
    zero = jnp.zeros_like(q)
    qs_ref[0:tq, :] = jnp.where(lane < DA_HEAD_DIM, q, zero)
    qs_ref[tq:2 * tq, :] = jnp.where(lane >= DA_HEAD_DIM, q, zero)
    m_ref[...] = jnp.full(m_ref.shape, -1e30, F32)
    acc_ref[...] = jnp.zeros(acc_ref.shape, F32)
    rb_rows = min(ATTN_RB, tq)
    nlc = tq // LANES

    def step(j, masked):
        off = pl.multiple_of(j * tq, tq)
        kt = k_ref[0, pl.ds(off, tq), :]
        vt = v_ref[0, pl.ds(off, tq), :]
        for rb in range(2 * tq // rb_rows):
            rs = slice(rb * rb_rows, (rb + 1) * rb_rows)
            s = lax.dot_general(qs_ref[rs, :], kt, (((1,), (1,)), ((), ())),
                                preferred_element_type=F32)
            if masked:
                qpos = (rb * rb_rows) % tq + lax.broadcasted_iota(jnp.int32, s.shape, 0)
                kpos = lax.broadcasted_iota(jnp.int32, s.shape, 1)
                s = jnp.where(kpos // CHUNK <= qpos // CHUNK, s, -1e30)
            m_old = m_ref[rs, :]
            m_new = jnp.maximum(m_old, jnp.max(s, axis=-1, keepdims=True))
            alpha = jnp.exp(m_old - m_new)
            p = jnp.exp(s - jnp.concatenate([m_new] * nlc, axis=1)).astype(BF16)
            pv = jnp.dot(p, vt, preferred_element_type=F32)
            acc_ref[rs, :] = acc_ref[rs, :] * jnp.concatenate([alpha, alpha], axis=1) + pv
            m_ref[rs, :] = m_new

    def body(j, carry):
        step(j, False)
        return carry

    lax.fori_loop(0, qi, body, 0)
    step(qi, True)

    lam = (jnp.exp(jnp.sum(lq1_ref[...] * lk1_ref[...], axis=-1, keepdims=True))
           - jnp.exp(jnp.sum(lq2_ref[...] * lk2_ref[...], axis=-1, keepdims=True)) + LAM_INIT)
    o1 = acc_ref[0:tq, 0:DA_V_DIM] / acc_ref[0:tq, DA_V_DIM:2 * DA_V_DIM]
    o2 = acc_ref[tq:2 * tq, 0:DA_V_DIM] / acc_ref[tq:2 * tq, DA_V_DIM:2 * DA_V_DIM]
    o = o1 - lam * o2
    o = _rms(o) * hg_ref[...] * (1.0 - LAM_INIT)
    o_ref[0] = o.astype(BF16)


def _attn(q, k, v, lq1, lk1, lq2, lk2, head_g):
    bsz, seq, _ = q.shape
    tq = min(ATTN_TQ, seq)
    lam_spec = pl.BlockSpec((1, DA_HEAD_DIM), lambda b, h, i: (0, 0))
    return pl.pallas_call(
        _attn_kernel,
        grid=(bsz, DA_HEADS, seq // tq),
        in_specs=[lam_spec, lam_spec, lam_spec, lam_spec,
                  pl.BlockSpec((1, tq, LANES), lambda b, h, i: (b, i, h)),
                  pl.BlockSpec((1, seq, LANES), lambda b, h, i: (b, 0, h)),
                  pl.BlockSpec((1, seq, 2 * DA_V_DIM), lambda b, h, i: (b, 0, h)),
                  pl.BlockSpec((1, DA_V_DIM), lambda b, h, i: (0, 0))],
        out_specs=pl.BlockSpec((1, tq, LANES), lambda b, h, i: (b, i, h)),
        out_shape=jax.ShapeDtypeStruct((bsz, seq, DA_WIDTH), BF16),
        scratch_shapes=[pltpu.VMEM((2 * tq, LANES), BF16),
                        pltpu.VMEM((2 * tq, LANES), F32),
                        pltpu.VMEM((2 * tq, 2 * DA_V_DIM), F32)],
        compiler_params=pltpu.CompilerParams(vmem_limit_bytes=VMEM_LIMIT),
        name="attn",
    )(lq1.reshape(1, -1), lk1.reshape(1, -1), lq2.reshape(1, -1), lk2.reshape(1, -1),
      q, k, v, head_g.reshape(1, -1))


def _merge_kernel(x_ref, mod_ref, g1n_ref, ya_ref, yb_ref, wg_ref, wa_ref, wb_ref, wo_ref,
                  g2n_ref, wq_ref, keys_ref, x1_ref, h2_ref, sct_ref):
    x = x_ref[...]
    sh1 = mod_ref[0, 0:1, :]
    sc1 = mod_ref[0, 1:2, :]
    g1 = mod_ref[0, 2:3, :]
    sh2 = mod_ref[0, 3:4, :]
    sc2 = mod_ref[0, 4:5, :]
    d = x.shape[1]
    h = (_rms(x) * g1n_ref[...] * (1.0 + sc1) + sh1).astype(BF16)
    ga = jax.nn.sigmoid(jnp.dot(h, wg_ref[:, 0:d], preferred_element_type=F32))
    a = jnp.dot(ya_ref[...], wa_ref[...], preferred_element_type=F32)
    merged = ga * a
    gb = jax.nn.sigmoid(jnp.dot(h, wg_ref[:, d:2 * d], preferred_element_type=F32))
    b = jnp.dot(yb_ref[...], wb_ref[...], preferred_element_type=F32)
    merged = (merged + gb * b).astype(BF16)
    x1 = x + g1 * jnp.dot(merged, wo_ref[...], preferred_element_type=F32)
    x1_ref[...] = x1
    h2 = _rms(x1) * g2n_ref[...] * (1.0 + sc2) + sh2
    h2_ref[...] = h2
    h2b = h2.astype(BF16)
    for hp in range(2 * PEER_HEADS):
        cs = slice(hp * PEER_HALF, (hp + 1) * PEER_HALF)
        qp = jnp.dot(h2b, wq_ref[:, cs], preferred_element_type=F32).astype(BF16)
        sct_ref[hp] = lax.dot_general(keys_ref[hp], qp, (((1,), (1,)), ((), ())),
                                      preferred_element_type=F32)


def _merge(xf, mod6, norm1_g, ya, yb, wg, wa, wb, wo, norm2_g, wq, keys, seq):
    n, d = xf.shape
    tm = min(MERGE_TM, seq)
    tiles_per_batch = seq // tm
    row = lambda i: (i, 0)
    const2 = lambda i: (0, 0)
    return pl.pallas_call(
        _merge_kernel,
        grid=(n // tm,),
        in_specs=[pl.BlockSpec((tm, d), row),
                  pl.BlockSpec((1, 6, d), lambda i: (i // tiles_per_batch, 0, 0)),
                  pl.BlockSpec((1, d), const2),
                  pl.BlockSpec((tm, DA_WIDTH), row),
                  pl.BlockSpec((tm, SG_WIDTH), row),
                  pl.BlockSpec(wg.shape, const2),
                  pl.BlockSpec(wa.shape, const2),
                  pl.BlockSpec(wb.shape, const2),
                  pl.BlockSpec(wo.shape, const2),
                  pl.BlockSpec((1, d), const2),
                  pl.BlockSpec(wq.shape, const2),
                  pl.BlockSpec(keys.shape, lambda i: (0, 0, 0))],
        out_specs=[pl.BlockSpec((tm, d), row),
                   pl.BlockSpec((tm, d), row),
                   pl.BlockSpec((2 * PEER_HEADS, PEER_KEYS, tm), lambda i: (0, 0, i))],
        out_shape=[jax.ShapeDtypeStruct((n, d), F32),
                   jax.ShapeDtypeStruct((n, d), F32),
                   jax.ShapeDtypeStruct((2 * PEER_HEADS, PEER_KEYS, n), F32)],
        compiler_params=pltpu.CompilerParams(vmem_limit_bytes=VMEM_LIMIT),
        name="merge",
    )(xf, mod6, norm1_g.reshape(1, d), ya, yb, wg, wa, wb, wo, norm2_g.reshape(1, d), wq, keys)


_CAND = [(i, j) for i in range(PEER_TOPK) for j in range(PEER_TOPK) if (i + 1) * (j + 1) <= PEER_TOPK]
_TOPK_CHUNK = 16


def _best_of(items):
    while len(items) > 1:
        nxt = []
        for a in range(0, len(items) - 1, 2):
            left, right = items[a], items[a + 1]
            keep = left[0] >= right[0]
            nxt.append(tuple(jnp.where(keep, x, y) for x, y in zip(left, right)))
        if len(items) % 2:
            nxt.append(items[-1])
        items = nxt
    return items[0]


def _topk_kernel(s_ref, e_ref, g_ref, w_ref, v_ref, i_ref):
    neg = jnp.float32(-jnp.inf)
    nkeys = w_ref.shape[1]
    w_ref[...] = s_ref[0]

    def body(it, prevs):
        outs = []
        for half in range(2):
            prev = prevs[half]
            partials = []
            for c0 in range(0, nkeys, _TOPK_CHUNK):
                items = []
                for k in range(c0, c0 + _TOPK_CHUNK):
                    s = jnp.where(prev == float(k), neg, w_ref[half, k])
                    w_ref[half, k] = s
                    items.append((s, float(k)))
                partials.append(_best_of(items))
            m, idx = _best_of(partials)
            v_ref[half, it] = m
            i_ref[half, it] = idx.astype(jnp.int32)
            outs.append(idx)
        return tuple(outs)

    none = jnp.full(w_ref.shape[2:], -1.0, F32)
    lax.fori_loop(0, PEER_TOPK, body, (none, none))

    cand = [v_ref[0, i] + v_ref[1, j] for (i, j) in _CAND]
    cid = [i_ref[0, i] * PEER_KEYS + i_ref[1, j] for (i, j) in _CAND]
    flat = [float(i * PEER_TOPK + j) for (i, j) in _CAND]
    tops = []
    pos = None
    for it in range(PEER_TOPK):
        if pos is not None:
            cand = [jnp.where(pos == f, neg, c) for c, f in zip(cand, flat)]
        m, pos, e = _best_of(list(zip(cand, flat, cid)))
        tops.append(m)
        e_ref[0, it] = e
    ex = [jnp.exp(t - tops[0]) for t in tops]
    tot = ex[0]
    for t in ex[1:]:
        tot = tot + t
    for it in range(PEER_TOPK):
        g_ref[0, it] = ex[it] / tot


def _topk(sct5):
    heads, _, keys, ng, _ = sct5.shape
    tg = min(TOPK_TG, ng)
    blk = (1, PEER_TOPK, tg, LANES)
    return pl.pallas_call(
        _topk_kernel,
        grid=(heads, ng // tg),
        in_specs=[pl.BlockSpec((1, 2, keys, tg, LANES), lambda h, t: (h, 0, 0, t, 0))],
        out_specs=[pl.BlockSpec(blk, lambda h, t: (h, 0, t, 0)),
                   pl.BlockSpec(blk, lambda h, t: (h, 0, t, 0))],
        out_shape=[jax.ShapeDtypeStruct((heads, PEER_TOPK, ng, LANES), jnp.int32),
                   jax.ShapeDtypeStruct((heads, PEER_TOPK, ng, LANES), F32)],
        scratch_shapes=[pltpu.VMEM((2, keys, tg, LANES), F32),
                        pltpu.VMEM((2, PEER_TOPK, tg, LANES), F32),
                        pltpu.VMEM((2, PEER_TOPK, tg, LANES), jnp.int32)],
        compiler_params=pltpu.CompilerParams(vmem_limit_bytes=VMEM_LIMIT),
        name="topk",
    )(sct5)


_NSLOT = 4
_AHEAD = 2
_BLOCK_PAIRS = PEER_TB * PEER_PAIRS


def _peer_kernel(idx01_ref, idxn_ref, tab_ref, gate_ref, h2_ref, x1_ref, g2_ref, fg_ref, o_ref,
                 buf0_ref, buf1_ref, buf2_ref, buf3_ref, wb_ref, sem_ref):
    bufs = (buf0_ref, buf1_ref, buf2_ref, buf3_ref)
    i = pl.program_id(0)
    nsteps = pl.num_programs(0)

    def issue(idx_ref, row0, s):
        for t in range(PEER_TB):
            issue_token(idx_ref, row0, t, s)

    def issue_token(idx_ref, row0, t, s):
        for j in range(PEER_PAIRS):
            pltpu.make_async_copy(tab_ref.at[idx_ref[row0 + t, j]], bufs[s].at[t * PEER_PAIRS + j],
                                  sem_ref.at[s]).start(priority=j % 2)

    def wait_block(s):
        pltpu.make_async_copy(bufs[s], bufs[s], sem_ref.at[s]).wait()

    @pl.when(i == 0)
    def _():
        for s in range(_AHEAD):
            issue(idx01_ref, s * PEER_TB, s)

    lane = lax.broadcasted_iota(jnp.int32, (SUBLANES, LANES), 1)
    g2 = g2_ref[0]
    fg = fg_ref[...]
    for s in range(_NSLOT):
        buf = bufs[s]
        nxt = (s + _AHEAD) % _NSLOT
        tok0 = s * PEER_TB
        wait_block(s)
        a_rows = []
        for t in range(PEER_TB):
            issue_token(idxn_ref, tok0, t, nxt)
            hv = h2_ref[tok0 + t]
            part = jnp.zeros((SUBLANES, LANES), F32)
            for j in range(PEER_PAIRS):
                prod = buf[t * PEER_PAIRS + j, 0:SUBLANES, :] * hv
                part = jnp.where(lane == j, jnp.sum(prod, axis=1, keepdims=True), part)
            a_rows.append(jnp.sum(part, axis=0, keepdims=True))
        a = jnp.concatenate(a_rows, axis=0)
        w = gate_ref[tok0:tok0 + PEER_TB, :] * jax.nn.gelu(a)
        for j in range(PEER_PAIRS):
            wb_ref[j] = jnp.broadcast_to(w[:, j:j + 1], (PEER_TB, LANES))
        for t in range(PEER_TB):
            y = jnp.zeros((SUBLANES, LANES), F32)
            for j in range(PEER_PAIRS):
                up = buf[t * PEER_PAIRS + j, SUBLANES:SLAB, :]
                y = y + up * jnp.broadcast_to(wb_ref[j, t:t + 1, :], (SUBLANES, LANES))
            x2 = x1_ref[tok0 + t] + g2 * y
            ms = jnp.sum(jnp.sum(x2 * x2, axis=1, keepdims=True), axis=0, keepdims=True)
            o_ref[tok0 + t] = x2 * lax.rsqrt(ms / (SUBLANES * LANES) + EPS) * fg

    @pl.when(i == nsteps - 1)
    def _():
        for s in range(_AHEAD):
            wait_block(s)


def _peer(eidx, gate, tab, h2r, x1r, g2r, fgr, seq):
    n = eidx.shape[0]
    tb = PEER_TB
    tstep = _NSLOT * tb
    nsteps = n // tstep
    assert n % tstep == 0 and seq % tstep == 0 and tb == SUBLANES
    steps_per_batch = seq // tstep
    eidx_ahead = jnp.concatenate([eidx[_AHEAD * tb:], eidx[n - _AHEAD * tb:]], axis=0)
    tok = lambda i: (i, 0, 0)
    slot_buf = pltpu.VMEM((_BLOCK_PAIRS, SLAB, LANES), F32)
    return pl.pallas_call(
        _peer_kernel,
        grid=(nsteps,),
        in_specs=[pl.BlockSpec((_AHEAD * tb, PEER_PAIRS), lambda i: (0, 0), memory_space=pltpu.SMEM),
                  pl.BlockSpec((tstep, PEER_PAIRS), lambda i: (i, 0), memory_space=pltpu.SMEM),
                  pl.BlockSpec(memory_space=pl.ANY),
                  pl.BlockSpec((tstep, PEER_PAIRS), lambda i: (i, 0)),
                  pl.BlockSpec((tstep, SUBLANES, LANES), tok),
                  pl.BlockSpec((tstep, SUBLANES, LANES), tok),
                  pl.BlockSpec((1, SUBLANES, LANES), lambda i: (i // steps_per_batch, 0, 0)),
                  pl.BlockSpec((SUBLANES, LANES), lambda i: (0, 0))],
        out_specs=pl.BlockSpec((tstep, SUBLANES, LANES), tok),
        out_shape=jax.ShapeDtypeStruct((n, SUBLANES, LANES), F32),
        scratch_shapes=[slot_buf] * _NSLOT + [pltpu.VMEM((PEER_PAIRS, tb, LANES), F32),
                                              pltpu.SemaphoreType.DMA((_NSLOT,))],
        compiler_params=pltpu.CompilerParams(dimension_semantics=("arbitrary",),
                                             vmem_limit_bytes=VMEM_LIMIT),
        name="peer",
    )(eidx, eidx_ahead, tab, gate, h2r, x1r, g2r, fgr)


def kernel(x, c, w_ada, b_ada, norm1_g, w_in, da_lambda_q1, da_lambda_k1, da_lambda_q2, da_lambda_k2, da_head_g, sg_ln_g, sg_ln_b, sg_w, sg_b, w_branch_a, w_branch_b, w_out, norm2_g, peer_w_query, peer_sub_keys, peer_down, peer_up, final_g):
    bsz, seq, d = x.shape
    n = bsz * seq
    assert w_ada.shape[0] == 1 and d == SUBLANES * LANES
    xf = x.reshape(n, d)

    mod6 = _ada(c, w_ada[0], b_ada[0]).reshape(bsz, 6, d)

    nproj = 2 * DA_QK_COLS + DA_WIDTH + 2 * SG_WIDTH
    w1 = w_in[0][:, :nproj].astype(BF16)
    wg = w_in[0][:, nproj:].astype(BF16)
    q, k, v, yb = _proj(xf, mod6, norm1_g[0], w1, sg_ln_g[0], sg_ln_b[0], sg_w[0], sg_b[0], seq)

    ya = _attn(q.reshape(bsz, seq, -1), k.reshape(bsz, seq, -1), v.reshape(bsz, seq, -1),
               da_lambda_q1[0], da_lambda_k1[0], da_lambda_q2[0], da_lambda_k2[0], da_head_g[0])

    keys = peer_sub_keys[0].reshape(2 * PEER_HEADS, PEER_KEYS, PEER_HALF).astype(BF16)
    x1, h2, sct = _merge(xf, mod6, norm1_g[0], ya.reshape(n, -1), yb, wg,
                         w_branch_a[0].astype(BF16), w_branch_b[0].astype(BF16),
                         w_out[0].astype(BF16), norm2_g[0], peer_w_query[0].astype(BF16), keys, seq)

    sct5 = sct.reshape(PEER_HEADS, 2, PEER_KEYS, n // LANES, LANES)
    eidx_t, gate = _topk(sct5)
    eidx = eidx_t.reshape(PEER_PAIRS, n).T
    gate = gate.reshape(PEER_PAIRS, n).T

    tab = jnp.concatenate([peer_down[0], peer_up[0]], axis=1).reshape(-1, SLAB, LANES)
    out = _peer(eidx, gate, tab,
                h2.reshape(n, SUBLANES, LANES), x1.reshape(n, SUBLANES, LANES),
                mod6[:, 5].reshape(bsz, SUBLANES, LANES), final_g.reshape(SUBLANES, LANES), seq)
    return out.reshape(bsz, seq, d)
```

```python
import functools

import jax
import jax.numpy as jnp
from jax import lax
from jax.experimental import pallas as pl
from jax.experimental.pallas import tpu as pltpu

F32 = jnp.float32
BF16 = jnp.bfloat16

EPS = 1e-6
CHUNK = 64
DA_HEADS = 4
DA_HEAD_DIM = 64
DA_V_DIM = 128
DA_QK_COLS = 512
DA_WIDTH = 512
SG_GROUPS = 4
SG_BLOCK = 128
SG_WIDTH = 512
PEER_HEADS = 8
PEER_KEYS = 128
PEER_HALF = 128
PEER_TOPK = 16
LAM_INIT = 0.8 - 0.6 * 1.0

LANES = 128
SUBLANES = 8

PROJ_TM = 512
MERGE_TM = 256
ATTN_TQ = 512
ATTN_RB = 128
TOPK_TG = 8
PEER_TB = 8
PEER_PAIRS = PEER_HEADS * PEER_TOPK
SLAB = 2 * SUBLANES
VMEM_LIMIT = 56 * 1024 * 1024


def _rms(x):
    return x * lax.rsqrt(jnp.mean(x * x, axis=-1, keepdims=True) + EPS)


def _ada_kernel(c_ref, w_ref, b_ref, o_ref):
    c = c_ref[...]
    s = (c * jax.nn.sigmoid(c)).astype(BF16)
    o_ref[...] = jnp.dot(s, w_ref[...].astype(BF16), preferred_element_type=F32) + b_ref[...]


def _ada(c, w, b):
    bsz, d = c.shape
    e = w.shape[1]
    tn = 1536
    return pl.pallas_call(
        _ada_kernel,
        grid=(e // tn,),
        in_specs=[pl.BlockSpec((bsz, d), lambda j: (0, 0)),
                  pl.BlockSpec((d, tn), lambda j: (0, j)),
                  pl.BlockSpec((1, tn), lambda j: (0, j))],
        out_specs=pl.BlockSpec((bsz, tn), lambda j: (0, j)),
        out_shape=jax.ShapeDtypeStruct((bsz, e), F32),
        name="ada",
    )(c, w, b.reshape(1, e))


def _proj_kernel(x_ref, mod_ref, g_ref, w_ref, lng_ref, lnb_ref, sgw_ref, sgb_ref,
                 q_ref, k_ref, v_ref, yb_ref):
    x = x_ref[...]
    sh1 = mod_ref[0, 0:1, :]
    sc1 = mod_ref[0, 1:2, :]
    h = (_rms(x) * g_ref[...] * (1.0 + sc1) + sh1).astype(BF16)

    def seg(a, b):
        return jnp.dot(h, w_ref[:, a:b], preferred_element_type=F32)

    q_ref[...] = (seg(0, 512) * (DA_HEAD_DIM ** -0.5)).astype(BF16)
    k_ref[...] = seg(512, 1024).astype(BF16)
    vv = seg(1024, 1536).astype(BF16)
    ones = jnp.ones((vv.shape[0], DA_V_DIM), BF16)
    for hh in range(DA_HEADS):
        v_ref[:, 2 * hh * DA_V_DIM:(2 * hh + 1) * DA_V_DIM] = vv[:, hh * DA_V_DIM:(hh + 1) * DA_V_DIM]
        v_ref[:, (2 * hh + 1) * DA_V_DIM:(2 * hh + 2) * DA_V_DIM] = ones
    u = jax.nn.gelu(seg(1536, 2048))
    sv = jax.nn.gelu(seg(2048, 2560))
    mu = jnp.mean(sv, axis=-1, keepdims=True)
    svc = sv - mu
    svn = svc * lax.rsqrt(jnp.mean(svc * svc, axis=-1, keepdims=True) + EPS)
    svn = (svn * lng_ref[...] + lnb_ref[...]).astype(BF16)
    pr = lax.broadcasted_iota(jnp.int32, (SG_BLOCK, SG_BLOCK), 0) // CHUNK
    pc = lax.broadcasted_iota(jnp.int32, (SG_BLOCK, SG_BLOCK), 1) // CHUNK
    keep = pr >= pc
    tm = x.shape[0]
    for g in range(SG_GROUPS):
        wm = jnp.where(keep, sgw_ref[g], 0.0).astype(BF16)
        bias = sgb_ref[g]
        cs = slice(g * SG_BLOCK, (g + 1) * SG_BLOCK)
        for r in range(tm // SG_BLOCK):
            rs = slice(r * SG_BLOCK, (r + 1) * SG_BLOCK)
            mixed = jnp.dot(wm, svn[rs, cs], preferred_element_type=F32) + bias
            yb_ref[rs, cs] = (u[rs, cs] * mixed).astype(BF16)


def _proj(xf, mod6, norm_g, w1, ln_g, ln_b, sg_w, sg_b, seq):
    n, d = xf.shape
    tm = min(PROJ_TM, seq)
    tiles_per_batch = seq // tm
    row = lambda i: (i, 0)
    const2 = lambda i: (0, 0)
    const3 = lambda i: (0, 0, 0)
    out = jax.ShapeDtypeStruct((n, 512), BF16)
    return pl.pallas_call(
        _proj_kernel,
        grid=(n // tm,),
        in_specs=[pl.BlockSpec((tm, d), row),
                  pl.BlockSpec((1, 6, d), lambda i: (i // tiles_per_batch, 0, 0)),
                  pl.BlockSpec((1, d), const2),
                  pl.BlockSpec(w1.shape, const2),
                  pl.BlockSpec((1, SG_WIDTH), const2),
                  pl.BlockSpec((1, SG_WIDTH), const2),
                  pl.BlockSpec((SG_GROUPS, SG_BLOCK, SG_BLOCK), const3),
                  pl.BlockSpec((SG_GROUPS, SG_BLOCK, 1), const3)],
        out_specs=[pl.BlockSpec((tm, 512), row), pl.BlockSpec((tm, 512), row),
                   pl.BlockSpec((tm, 2 * DA_WIDTH), row), pl.BlockSpec((tm, 512), row)],
        out_shape=[out, out, jax.ShapeDtypeStruct((n, 2 * DA_WIDTH), BF16), out],
        compiler_params=pltpu.CompilerParams(vmem_limit_bytes=VMEM_LIMIT),
        name="proj",
    )(xf, mod6, norm_g.reshape(1, d), w1, ln_g.reshape(1, -1), ln_b.reshape(1, -1),
      sg_w, sg_b[..., None])


def _attn_kernel(lq1_ref, lk1_ref, lq2_ref, lk2_ref, q_ref, k_ref, v_ref, hg_ref, o_ref,
                 qs_ref, m_ref, acc_ref):
    tq = q_ref.shape[1]
    qi = pl.program_id(2)
    q = q_ref[0]
    lane = lax.broadcasted_iota(jnp.int32, q.shape, 1)
    zero = jnp.zeros_like(q)
    qs_ref[0:tq, :] = jnp.where(lane < DA_HEAD_DIM, q, zero)
    qs_ref[tq:2 * tq, :] = jnp.where(lane >= DA_HEAD_DIM, q, zero)
    m_ref[...] = jnp.full(m_ref.shape, -1e30, F32)
    acc_ref[...] = jnp.zeros(acc_ref.shape, F32)
    rb_rows = min(ATTN_RB, tq)
    nlc = tq // LANES

    def step(j, masked):
        off = pl.multiple_of(j * tq, tq)
        kt = k_ref[0, pl.ds(off, tq), :]
        vt = v_ref[0, pl.ds(off, tq), :]
        for rb in range(2 * tq // rb_rows):
            rs = slice(rb * rb_rows, (rb + 1) * rb_rows)
            s = lax.dot_general(qs_ref[rs, :], kt, (((1,), (1,)), ((), ())),
                                preferred_element_type=F32)
            if masked:
                qpos = (rb * rb_rows) % tq + lax.broadcasted_iota(jnp.int32, s.shape, 0)
                kpos = lax.broadcasted_iota(jnp.int32, s.shape, 1)
                s = jnp.where(kpos // CHUNK <= qpos // CHUNK, s, -1e30)
            m_old = m_ref[rs, :]
            m_new = jnp.maximum(m_old, jnp.max(s, axis=-1, keepdims=True))
            alpha = jnp.exp(m_old - m_new)
            p = jnp.exp(s - jnp.concatenate([m_new] * nlc, axis=1)).astype(BF16)
            pv = jnp.dot(p, vt, preferred_element_type=F32)
            acc_ref[rs, :] = acc_ref[rs, :] * jnp.concatenate([alpha, alpha], axis=1) + pv
            m_ref[rs, :] = m_new

    def body(j, carry):
        step(j, False)
        return carry

    lax.fori_loop(0, qi, body, 0)
    step(qi, True)

    lam = (jnp.exp(jnp.sum(lq1_ref[...] * lk1_ref[...], axis=-1, keepdims=True))
           - jnp.exp(jnp.sum(lq2_ref[...] * lk2_ref[...], axis=-1, keepdims=True)) + LAM_INIT)
    o1 = acc_ref[0:tq, 0:DA_V_DIM] / acc_ref[0:tq, DA_V_DIM:2 * DA_V_DIM]
    o2 = acc_ref[tq:2 * tq, 0:DA_V_DIM] / acc_ref[tq:2 * tq, DA_V_DIM:2 * DA_V_DIM]
    o = o1 - lam * o2
    o = _rms(o) * hg_ref[...] * (1.0 - LAM_INIT)
    o_ref[0] = o.astype(BF16)


def _attn(q, k, v, lq1, lk1, lq2, lk2, head_g):
    bsz, seq, _ = q.shape
    tq = min(ATTN_TQ, seq)
    lam_spec = pl.BlockSpec((1, DA_HEAD_DIM), lambda b, h, i: (0, 0))
    return pl.pallas_call(
        _attn_kernel,
        grid=(bsz, DA_HEADS, seq // tq),
        in_specs=[lam_spec, lam_spec, lam_spec, lam_spec,
                  pl.BlockSpec((1, tq, LANES), lambda b, h, i: (b, i, h)),
                  pl.BlockSpec((1, seq, LANES), lambda b, h, i: (b, 0, h)),
                  pl.BlockSpec((1, seq, 2 * DA_V_DIM), lambda b, h, i: (b, 0, h)),
                  pl.BlockSpec((1, DA_V_DIM), lambda b, h, i: (0, 0))],
        out_specs=pl.BlockSpec((1, tq, LANES), lambda b, h, i: (b, i, h)),
        out_shape=jax.ShapeDtypeStruct((bsz, seq, DA_WIDTH), BF16),
        scratch_shapes=[pltpu.VMEM((2 * tq, LANES), BF16),
                        pltpu.VMEM((2 * tq, LANES), F32),
                        pltpu.VMEM((2 * tq, 2 * DA_V_DIM), F32)],
        compiler_params=pltpu.CompilerParams(vmem_limit_bytes=VMEM_LIMIT),
        name="attn",
    )(lq1.reshape(1, -1), lk1.reshape(1, -1), lq2.reshape(1, -1), lk2.reshape(1, -1),
      q, k, v, head_g.reshape(1, -1))


def _merge_kernel(x_ref, mod_ref, g1n_ref, ya_ref, yb_ref, wg_ref, wa_ref, wb_ref, wo_ref,
                  g2n_ref, wq_ref, keys_ref, x1_ref, h2_ref, sct_ref):
    x = x_ref[...]
    sh1 = mod_ref[0, 0:1, :]
    sc1 = mod_ref[0, 1:2, :]
    g1 = mod_ref[0, 2:3, :]
    sh2 = mod_ref[0, 3:4, :]
    sc2 = mod_ref[0, 4:5, :]
    d = x.shape[1]
    h = (_rms(x) * g1n_ref[...] * (1.0 + sc1) + sh1).astype(BF16)
    ga = jax.nn.sigmoid(jnp.dot(h, wg_ref[:, 0:d], preferred_element_type=F32))
    a = jnp.dot(ya_ref[...], wa_ref[...], preferred_element_type=F32)
    merged = ga * a
    gb = jax.nn.sigmoid(jnp.dot(h, wg_ref[:, d:2 * d], preferred_element_type=F32))
    b = jnp.dot(yb_ref[...], wb_ref[...], preferred_element_type=F32)
    merged = (merged + gb * b).astype(BF16)
    x1 = x + g1 * jnp.dot(merged, wo_ref[...], preferred_element_type=F32)
    x1_ref[...] = x1
    h2 = _rms(x1) * g2n_ref[...] * (1.0 + sc2) + sh2
    h2_ref[...] = h2
    h2b = h2.astype(BF16)
    for hp in range(2 * PEER_HEADS):
        cs = slice(hp * PEER_HALF, (hp + 1) * PEER_HALF)
        qp = jnp.dot(h2b, wq_ref[:, cs], preferred_element_type=F32).astype(BF16)
        sct_ref[hp] = lax.dot_general(keys_ref[hp], qp, (((1,), (1,)), ((), ())),
                                      preferred_element_type=F32)


def _merge(xf, mod6, norm1_g, ya, yb, wg, wa, wb, wo, norm2_g, wq, keys, seq):
    n, d = xf.shape
    tm = min(MERGE_TM, seq)
    tiles_per_batch = seq // tm
    row = lambda i: (i, 0)
    const2 = lambda i: (0, 0)
    return pl.pallas_call(
        _merge_kernel,
        grid=(n // tm,),
        in_specs=[pl.BlockSpec((tm, d), row),
                  pl.BlockSpec((1, 6, d), lambda i: (i // tiles_per_batch, 0, 0)),
                  pl.BlockSpec((1, d), const2),
                  pl.BlockSpec((tm, DA_WIDTH), row),
                  pl.BlockSpec((tm, SG_WIDTH), row),
                  pl.BlockSpec(wg.shape, const2),
                  pl.BlockSpec(wa.shape, const2),
                  pl.BlockSpec(wb.shape, const2),
                  pl.BlockSpec(wo.shape, const2),
                  pl.BlockSpec((1, d), const2),
                  pl.BlockSpec(wq.shape, const2),
                  pl.BlockSpec(keys.shape, lambda i: (0, 0, 0))],
        out_specs=[pl.BlockSpec((tm, d), row),
                   pl.BlockSpec((tm, d), row),
                   pl.BlockSpec((2 * PEER_HEADS, PEER_KEYS, tm), lambda i: (0, 0, i))],
        out_shape=[jax.ShapeDtypeStruct((n, d), F32),
                   jax.ShapeDtypeStruct((n, d), F32),
                   jax.ShapeDtypeStruct((2 * PEER_HEADS, PEER_KEYS, n), F32)],
        compiler_params=pltpu.CompilerParams(vmem_limit_bytes=VMEM_LIMIT),
        name="merge",
    )(xf, mod6, norm1_g.reshape(1, d), ya, yb, wg, wa, wb, wo, norm2_g.reshape(1, d), wq, keys)


_CAND = [(i, j) for i in range(PEER_TOPK) for j in range(PEER_TOPK) if (i + 1) * (j + 1) <= PEER_TOPK]
_TOPK_CHUNK = 16


def _best_of(items):
    while len(items) > 1:
        nxt = []
        for a in range(0, len(items) - 1, 2):
            left, right = items[a], items[a + 1]
            keep = left[0] >= right[0]
            nxt.append(tuple(jnp.where(keep, x, y) for x, y in zip(left, right)))
        if len(items) % 2:
            nxt.append(items[-1])
        items = nxt
    return items[0]


def _topk_kernel(s_ref, e_ref, g_ref, w_ref, v_ref, i_ref):
    neg = jnp.float32(-jnp.inf)
    nkeys = w_ref.shape[1]
    w_ref[...] = s_ref[0]

    def body(it, prevs):
        outs = []
        for half in range(2):
            prev = prevs[half]
            partials = []
            for c0 in range(0, nkeys, _TOPK_CHUNK):
                items = []
                for k in range(c0, c0 + _TOPK_CHUNK):
                    s = jnp.where(prev == float(k), neg, w_ref[half, k])
                    w_ref[half, k] = s
                    items.append((s, float(k)))
                partials.append(_best_of(items))
            m, idx = _best_of(partials)
            v_ref[half, it] = m
            i_ref[half, it] = idx.astype(jnp.int32)
            outs.append(idx)
        return tuple(outs)

    none = jnp.full(w_ref.shape[2:], -1.0, F32)
    lax.fori_loop(0, PEER_TOPK, body, (none, none))

    cand = [v_ref[0, i] + v_ref[1, j] for (i, j) in _CAND]
    cid = [i_ref[0, i] * PEER_KEYS + i_ref[1, j] for (i, j) in _CAND]
    flat = [float(i * PEER_TOPK + j) for (i, j) in _CAND]
    tops = []
    pos = None
    for it in range(PEER_TOPK):
        if pos is not None:
            cand = [jnp.where(pos == f, neg, c) for c, f in zip(cand, flat)]
        m, pos, e = _best_of(list(zip(cand, flat, cid)))
        tops.append(m)
        e_ref[0, it] = e
    ex = [jnp.exp(t - tops[0]) for t in tops]
    tot = ex[0]
    for t in ex[1:]:
        tot = tot + t
    for it in range(PEER_TOPK):
        g_ref[0, it] = ex[it] / tot


def _topk(sct5):
    heads, _, keys, ng, _ = sct5.shape
    tg = min(TOPK_TG, ng)
    blk = (1, PEER_TOPK, tg, LANES)
    return pl.pallas_call(
        _topk_kernel,
        grid=(heads, ng // tg),
        in_specs=[pl.BlockSpec((1, 2, keys, tg, LANES), lambda h, t: (h, 0, 0, t, 0))],
        out_specs=[pl.BlockSpec(blk, lambda h, t: (h, 0, t, 0)),
                   pl.BlockSpec(blk, lambda h, t: (h, 0, t, 0))],
        out_shape=[jax.ShapeDtypeStruct((heads, PEER_TOPK, ng, LANES), jnp.int32),
                   jax.ShapeDtypeStruct((heads, PEER_TOPK, ng, LANES), F32)],
        scratch_shapes=[pltpu.VMEM((2, keys, tg, LANES), F32),
                        pltpu.VMEM((2, PEER_TOPK, tg, LANES), F32),
                        pltpu.VMEM((2, PEER_TOPK, tg, LANES), jnp.int32)],
        compiler_params=pltpu.CompilerParams(vmem_limit_bytes=VMEM_LIMIT),
        name="topk",
    )(sct5)


_NSLOT = 4
_AHEAD = 3
_BLOCK_PAIRS = PEER_TB * PEER_PAIRS


def _peer_kernel(idx01_ref, idxn_ref, tab_ref, gate_ref, h2_ref, x1_ref, g2_ref, fg_ref, o_ref,
                 buf0_ref, buf1_ref, buf2_ref, buf3_ref, wb_ref, sem_ref):
    bufs = (buf0_ref, buf1_ref, buf2_ref, buf3_ref)
    i = pl.program_id(0)
    nsteps = pl.num_programs(0)

    def issue(idx_ref, row0, s):
        for t in range(PEER_TB):
            issue_token(idx_ref, row0, t, s)

    def issue_token(idx_ref, row0, t, s):
        for j in range(PEER_PAIRS):
            pltpu.make_async_copy(tab_ref.at[idx_ref[row0 + t, j]], bufs[s].at[t * PEER_PAIRS + j],
                                  sem_ref.at[s]).start(priority=j % 2)

    def wait_block(s):
        pltpu.make_async_copy(bufs[s], bufs[s], sem_ref.at[s]).wait()

    @pl.when(i == 0)
    def _():
        for s in range(_AHEAD):
            issue(idx01_ref, s * PEER_TB, s)

    lane = lax.broadcasted_iota(jnp.int32, (SUBLANES, LANES), 1)
    g2 = g2_ref[0]
    fg = fg_ref[...]
    for s in range(_NSLOT):
        buf = bufs[s]
        nxt = (s + _AHEAD) % _NSLOT
        tok0 = s * PEER_TB
        wait_block(s)
        a_rows = []
        for t in range(PEER_TB):
            issue_token(idxn_ref, tok0, t, nxt)
            hv = h2_ref[tok0 + t]
            part = jnp.zeros((SUBLANES, LANES), F32)
            for j in range(PEER_PAIRS):
                prod = buf[t * PEER_PAIRS + j, 0:SUBLANES, :] * hv
                part = jnp.where(lane == j, jnp.sum(prod, axis=1, keepdims=True), part)
            a_rows.append(jnp.sum(part, axis=0, keepdims=True))
        a = jnp.concatenate(a_rows, axis=0)
        w = gate_ref[tok0:tok0 + PEER_TB, :] * jax.nn.gelu(a)
        for j in range(PEER_PAIRS):
            wb_ref[j] = jnp.broadcast_to(w[:, j:j + 1], (PEER_TB, LANES))
        for t in range(PEER_TB):
            y = jnp.zeros((SUBLANES, LANES), F32)
            for j in range(PEER_PAIRS):
                up = buf[t * PEER_PAIRS + j, SUBLANES:SLAB, :]
                y = y + up * jnp.broadcast_to(wb_ref[j, t:t + 1, :], (SUBLANES, LANES))
            x2 = x1_ref[tok0 + t] + g2 * y
            ms = jnp.sum(jnp.sum(x2 * x2, axis=1, keepdims=True), axis=0, keepdims=True)
            o_ref[tok0 + t] = x2 * lax.rsqrt(ms / (SUBLANES * LANES) + EPS) * fg

    @pl.when(i == nsteps - 1)
    def _():
        for s in range(_AHEAD):
            wait_block(s)


def _peer(eidx, gate, tab, h2r, x1r, g2r, fgr, seq):
    n = eidx.shape[0]
    tb = PEER_TB
    tstep = _NSLOT * tb
    nsteps = n // tstep
    assert n % tstep == 0 and seq % tstep == 0 and tb == SUBLANES
    steps_per_batch = seq // tstep
    eidx_ahead = jnp.concatenate([eidx[_AHEAD * tb:], eidx[n - _AHEAD * tb:]], axis=0)
    tok = lambda i: (i, 0, 0)
    slot_buf = pltpu.VMEM((_BLOCK_PAIRS, SLAB, LANES), F32)
    return pl.pallas_call(
        _peer_kernel,
        grid=(nsteps,),
        in_specs=[pl.BlockSpec((_AHEAD * tb, PEER_PAIRS), lambda i: (0, 0), memory_space=pltpu.SMEM),
                  pl.BlockSpec((tstep, PEER_PAIRS), lambda i: (i, 0), memory_space=pltpu.SMEM),
                  pl.BlockSpec(memory_space=pl.ANY),
                  pl.BlockSpec((tstep, PEER_PAIRS), lambda i: (i, 0)),
                  pl.BlockSpec((tstep, SUBLANES, LANES), tok),
                  pl.BlockSpec((tstep, SUBLANES, LANES), tok),
                  pl.BlockSpec((1, SUBLANES, LANES), lambda i: (i // steps_per_batch, 0, 0)),
                  pl.BlockSpec((SUBLANES, LANES), lambda i: (0, 0))],
        out_specs=pl.BlockSpec((tstep, SUBLANES, LANES), tok),
        out_shape=jax.ShapeDtypeStruct((n, SUBLANES, LANES), F32),
        scratch_shapes=[slot_buf] * _NSLOT + [pltpu.VMEM((PEER_PAIRS, tb, LANES), F32),
                                              pltpu.SemaphoreType.DMA((_NSLOT,))],
        compiler_params=pltpu.CompilerParams(dimension_semantics=("arbitrary",),
                                             vmem_limit_bytes=VMEM_LIMIT),
        name="peer",
    )(eidx, eidx_ahead, tab, gate, h2r, x1r, g2r, fgr)


def kernel(x, c, w_ada, b_ada, norm1_g, w_in, da_lambda_q1, da_lambda_k1, da_lambda_q2, da_lambda_k2, da_head_g, sg_ln_g, sg_ln_b, sg_w, sg_b, w_branch_a, w_branch_b, w_out, norm2_g, peer_w_query, peer_sub_keys, peer_down, peer_up, final_g):
    bsz, seq, d = x.shape
    n = bsz * seq
    assert w_ada.shape[0] == 1 and d == SUBLANES * LANES
    xf = x.reshape(n, d)

    mod6 = _ada(c, w_ada[0], b_ada[0]).reshape(bsz, 6, d)

    nproj = 2 * DA_QK_COLS + DA_WIDTH + 2 * SG_WIDTH
    w1 = w_in[0][:, :nproj].astype(BF16)
    wg = w_in[0][:, nproj:].astype(BF16)
    q, k, v, yb = _proj(xf, mod6, norm1_g[0], w1, sg_ln_g[0], sg_ln_b[0], sg_w[0], sg_b[0], seq)

    ya = _attn(q.reshape(bsz, seq, -1), k.reshape(bsz, seq, -1), v.reshape(bsz, seq, -1),
               da_lambda_q1[0], da_lambda_k1[0], da_lambda_q2[0], da_lambda_k2[0], da_head_g[0])

    keys = peer_sub_keys[0].reshape(2 * PEER_HEADS, PEER_KEYS, PEER_HALF).astype(BF16)
    x1, h2, sct = _merge(xf, mod6, norm1_g[0], ya.reshape(n, -1), yb, wg,
                         w_branch_a[0].astype(BF16), w_branch_b[0].astype(BF16),
                         w_out[0].astype(BF16), norm2_g[0], peer_w_query[0].astype(BF16), keys, seq)

    sct5 = sct.reshape(PEER_HEADS, 2, PEER_KEYS, n // LANES, LANES)
    eidx_t, gate = _topk(sct5)
    eidx = eidx_t.reshape(PEER_PAIRS, n).T
    gate = gate.reshape(PEER_PAIRS, n).T

    tab = jnp.concatenate([peer_down[0], peer_up[0]], axis=1).reshape(-1, SLAB, LANES)
    out = _peer(eidx, gate, tab,
                h2.reshape(n, SUBLANES, LANES), x1.reshape(n, SUBLANES, LANES),
                mod6[:, 5].reshape(bsz, SUBLANES, LANES), final_g.reshape(SUBLANES, LANES), seq)
    return out.reshape(bsz, seq, d)
```

```python
import functools

import jax
import jax.numpy as jnp
from jax import lax
from jax.experimental import pallas as pl
from jax.experimental.pallas import tpu as pltpu

F32 = jnp.float32
BF16 = jnp.bfloat16

EPS = 1e-6
CHUNK = 64
DA_HEADS = 4
DA_HEAD_DIM = 64
DA_V_DIM = 128
DA_QK_COLS = 512
DA_WIDTH = 512
SG_GROUPS = 4
SG_BLOCK = 128
SG_WIDTH = 512
PEER_HEADS = 8
PEER_KEYS = 128
PEER_HALF = 128
PEER_TOPK = 16
LAM_INIT = 0.8 - 0.6 * 1.0

LANES = 128
SUBLANES = 8

PROJ_TM = 512
MERGE_TM = 256
ATTN_TQ = 512
ATTN_RB = 256
ATTN_TK = 256
TOPK_TG = 8
PEER_TB = 8
PEER_PAIRS = PEER_HEADS * PEER_TOPK
SLAB = 2 * SUBLANES
VMEM_LIMIT = 56 * 1024 * 1024


def _rms(x):
    return x * lax.rsqrt(jnp.mean(x * x, axis=-1, keepdims=True) + EPS)


def _ada_kernel(c_ref, w_ref, b_ref, o_ref):
    c = c_ref[...]
    s = (c * jax.nn.sigmoid(c)).astype(BF16)
    o_ref[...] = jnp.dot(s, w_ref[...].astype(BF16), preferred_element_type=F32) + b_ref[...]


def _ada(c, w, b):
    bsz, d = c.shape
    e = w.shape[1]
    tn = 1536
    return pl.pallas_call(
        _ada_kernel,
        grid=(e // tn,),
        in_specs=[pl.BlockSpec((bsz, d), lambda j: (0, 0)),
                  pl.BlockSpec((d, tn), lambda j: (0, j)),
                  pl.BlockSpec((1, tn), lambda j: (0, j))],
        out_specs=pl.BlockSpec((bsz, tn), lambda j: (0, j)),
        out_shape=jax.ShapeDtypeStruct((bsz, e), F32),
        name="ada",
    )(c, w, b.reshape(1, e))


def _proj_kernel(x_ref, mod_ref, g_ref, w_ref, lng_ref, lnb_ref, sgw_ref, sgb_ref,
                 q_ref, k_ref, v_ref, yb_ref):
    x = x_ref[...]
    sh1 = mod_ref[0, 0:1, :]
    sc1 = mod_ref[0, 1:2, :]
    h = (_rms(x) * g_ref[...] * (1.0 + sc1) + sh1).astype(BF16)

    def seg(a, b):
        return jnp.dot(h, w_ref[:, a:b], preferred_element_type=F32)

    q_ref[...] = (seg(0, 512) * (DA_HEAD_DIM ** -0.5)).astype(BF16)
    k_ref[...] = seg(512, 1024).astype(BF16)
    vv = seg(1024, 1536).astype(BF16)
    ones = jnp.ones((vv.shape[0], DA_V_DIM), BF16)
    for hh in range(DA_HEADS):
        v_ref[:, 2 * hh * DA_V_DIM:(2 * hh + 1) * DA_V_DIM] = vv[:, hh * DA_V_DIM:(hh + 1) * DA_V_DIM]
        v_ref[:, (2 * hh + 1) * DA_V_DIM:(2 * hh + 2) * DA_V_DIM] = ones
    u = jax.nn.gelu(seg(1536, 2048))
    sv = jax.nn.gelu(seg(2048, 2560))
    mu = jnp.mean(sv, axis=-1, keepdims=True)
    svc = sv - mu
    svn = svc * lax.rsqrt(jnp.mean(svc * svc, axis=-1, keepdims=True) + EPS)
    svn = (svn * lng_ref[...] + lnb_ref[...]).astype(BF16)
    pr = lax.broadcasted_iota(jnp.int32, (SG_BLOCK, SG_BLOCK), 0) // CHUNK
    pc = lax.broadcasted_iota(jnp.int32, (SG_BLOCK, SG_BLOCK), 1) // CHUNK
    keep = pr >= pc
    tm = x.shape[0]
    for g in range(SG_GROUPS):
        wm = jnp.where(keep, sgw_ref[g], 0.0).astype(BF16)
        bias = sgb_ref[g]
        cs = slice(g * SG_BLOCK, (g + 1) * SG_BLOCK)
        for r in range(tm // SG_BLOCK):
            rs = slice(r * SG_BLOCK, (r + 1) * SG_BLOCK)
            mixed = jnp.dot(wm, svn[rs, cs], preferred_element_type=F32) + bias
            yb_ref[rs, cs] = (u[rs, cs] * mixed).astype(BF16)


def _proj(xf, mod6, norm_g, w1, ln_g, ln_b, sg_w, sg_b, seq):
    n, d = xf.shape
    tm = min(PROJ_TM, seq)
    tiles_per_batch = seq // tm
    row = lambda i: (i, 0)
    const2 = lambda i: (0, 0)
    const3 = lambda i: (0, 0, 0)
    out = jax.ShapeDtypeStruct((n, 512), BF16)
    return pl.pallas_call(
        _proj_kernel,
        grid=(n // tm,),
        in_specs=[pl.BlockSpec((tm, d), row),
                  pl.BlockSpec((1, 6, d), lambda i: (i // tiles_per_batch, 0, 0)),
                  pl.BlockSpec((1, d), const2),
                  pl.BlockSpec(w1.shape, const2),
                  pl.BlockSpec((1, SG_WIDTH), const2),
                  pl.BlockSpec((1, SG_WIDTH), const2),
                  pl.BlockSpec((SG_GROUPS, SG_BLOCK, SG_BLOCK), const3),
                  pl.BlockSpec((SG_GROUPS, SG_BLOCK, 1), const3)],
        out_specs=[pl.BlockSpec((tm, 512), row), pl.BlockSpec((tm, 512), row),
                   pl.BlockSpec((tm, 2 * DA_WIDTH), row), pl.BlockSpec((tm, 512), row)],
        out_shape=[out, out, jax.ShapeDtypeStruct((n, 2 * DA_WIDTH), BF16), out],
        compiler_params=pltpu.CompilerParams(vmem_limit_bytes=VMEM_LIMIT),
        name="proj",
    )(xf, mod6, norm_g.reshape(1, d), w1, ln_g.reshape(1, -1), ln_b.reshape(1, -1),
      sg_w, sg_b[..., None])


def _attn_kernel(lq1_ref, lk1_ref, lq2_ref, lk2_ref, q_ref, k_ref, v_ref, hg_ref, o_ref,
                 qs_ref, m_ref, acc_ref):
    tq = q_ref.shape[1]
    qi = pl.program_id(2)
    q = q_ref[0]
    lane = lax.broadcasted_iota(jnp.int32, q.shape, 1)
    zero = jnp.zeros_like(q)
    qs_ref[0:tq, :] = jnp.where(lane < DA_HEAD_DIM, q, zero)
    qs_ref[tq:2 * tq, :] = jnp.where(lane >= DA_HEAD_DIM, q, zero)
    m_ref[...] = jnp.full(m_ref.shape, -1e30, F32)
    acc_ref[...] = jnp.zeros(acc_ref.shape, F32)
    rb_rows = min(ATTN_RB, tq)
    tk = min(ATTN_TK, tq)
    nlc = tk // LANES

    def step(j, masked):
        for kc in range(tq // tk):
            off = pl.multiple_of(j * tq + kc * tk, tk)
            kt = k_ref[0, pl.ds(off, tk), :]
            vt = v_ref[0, pl.ds(off, tk), :]
            for rb in range(2 * tq // rb_rows):
                q0 = (rb * rb_rows) % tq
                if masked and q0 + rb_rows <= kc * tk:
                    continue
                rs = slice(rb * rb_rows, (rb + 1) * rb_rows)
                s = lax.dot_general(qs_ref[rs, :], kt, (((1,), (1,)), ((), ())),
                                    preferred_element_type=F32)
                if masked and q0 < (kc + 1) * tk:
                    qpos = q0 + lax.broadcasted_iota(jnp.int32, s.shape, 0)
                    kpos = kc * tk + lax.broadcasted_iota(jnp.int32, s.shape, 1)
                    s = jnp.where(kpos // CHUNK <= qpos // CHUNK, s, -1e30)
                m_old = m_ref[rs, :]
                m_new = jnp.maximum(m_old, jnp.max(s, axis=-1, keepdims=True))
                alpha = jnp.exp(m_old - m_new)
                p = jnp.exp(s - jnp.concatenate([m_new] * nlc, axis=1)).astype(BF16)
                pv = jnp.dot(p, vt, preferred_element_type=F32)
                acc_ref[rs, :] = acc_ref[rs, :] * jnp.concatenate([alpha, alpha], axis=1) + pv
                m_ref[rs, :] = m_new

    def body(j, carry):
        step(j, False)
        return carry

    lax.fori_loop(0, qi, body, 0)
    step(qi, True)

    lam = (jnp.exp(jnp.sum(lq1_ref[...] * lk1_ref[...], axis=-1, keepdims=True))
           - jnp.exp(jnp.sum(lq2_ref[...] * lk2_ref[...], axis=-1, keepdims=True)) + LAM_INIT)
    o1 = acc_ref[0:tq, 0:DA_V_DIM] / acc_ref[0:tq, DA_V_DIM:2 * DA_V_DIM]
    o2 = acc_ref[tq:2 * tq, 0:DA_V_DIM] / acc_ref[tq:2 * tq, DA_V_DIM:2 * DA_V_DIM]
    o = o1 - lam * o2
    o = _rms(o) * hg_ref[...] * (1.0 - LAM_INIT)
    o_ref[0] = o.astype(BF16)


def _attn(q, k, v, lq1, lk1, lq2, lk2, head_g):
    bsz, seq, _ = q.shape
    tq = min(ATTN_TQ, seq)
    lam_spec = pl.BlockSpec((1, DA_HEAD_DIM), lambda b, h, i: (0, 0))
    return pl.pallas_call(
        _attn_kernel,
        grid=(bsz, DA_HEADS, seq // tq),
        in_specs=[lam_spec, lam_spec, lam_spec, lam_spec,
                  pl.BlockSpec((1, tq, LANES), lambda b, h, i: (b, i, h)),
                  pl.BlockSpec((1, seq, LANES), lambda b, h, i: (b, 0, h)),
                  pl.BlockSpec((1, seq, 2 * DA_V_DIM), lambda b, h, i: (b, 0, h)),
                  pl.BlockSpec((1, DA_V_DIM), lambda b, h, i: (0, 0))],
        out_specs=pl.BlockSpec((1, tq, LANES), lambda b, h, i: (b, i, h)),
        out_shape=jax.ShapeDtypeStruct((bsz, seq, DA_WIDTH), BF16),
        scratch_shapes=[pltpu.VMEM((2 * tq, LANES), BF16),
                        pltpu.VMEM((2 * tq, LANES), F32),
                        pltpu.VMEM((2 * tq, 2 * DA_V_DIM), F32)],
        compiler_params=pltpu.CompilerParams(vmem_limit_bytes=VMEM_LIMIT),
        name="attn",
    )(lq1.reshape(1, -1), lk1.reshape(1, -1), lq2.reshape(1, -1), lk2.reshape(1, -1),
      q, k, v, head_g.reshape(1, -1))


def _merge_kernel(x_ref, mod_ref, g1n_ref, ya_ref, yb_ref, wg_ref, wa_ref, wb_ref, wo_ref,
                  g2n_ref, wq_ref, keys_ref, x1_ref, h2_ref, sct_ref):
    x = x_ref[...]
    sh1 = mod_ref[0, 0:1, :]
    sc1 = mod_ref[0, 1:2, :]
    g1 = mod_ref[0, 2:3, :]
    sh2 = mod_ref[0, 3:4, :]
    sc2 = mod_ref[0, 4:5, :]
    d = x.shape[1]
    h = (_rms(x) * g1n_ref[...] * (1.0 + sc1) + sh1).astype(BF16)
    ga = jax.nn.sigmoid(jnp.dot(h, wg_ref[:, 0:d], preferred_element_type=F32))
    a = jnp.dot(ya_ref[...], wa_ref[...], preferred_element_type=F32)
    merged = ga * a
    gb = jax.nn.sigmoid(jnp.dot(h, wg_ref[:, d:2 * d], preferred_element_type=F32))
    b = jnp.dot(yb_ref[...], wb_ref[...], preferred_element_type=F32)
    merged = (merged + gb * b).astype(BF16)
    x1 = x + g1 * jnp.dot(merged, wo_ref[...], preferred_element_type=F32)
    x1_ref[...] = x1
    h2 = _rms(x1) * g2n_ref[...] * (1.0 + sc2) + sh2
    h2_ref[...] = h2
    h2b = h2.astype(BF16)
    for hp in range(2 * PEER_HEADS):
        cs = slice(hp * PEER_HALF, (hp + 1) * PEER_HALF)
        qp = jnp.dot(h2b, wq_ref[:, cs], preferred_element_type=F32).astype(BF16)
        sct_ref[hp] = lax.dot_general(keys_ref[hp], qp, (((1,), (1,)), ((), ())),
                                      preferred_element_type=F32)


def _merge(xf, mod6, norm1_g, ya, yb, wg, wa, wb, wo, norm2_g, wq, keys, seq):
    n, d = xf.shape
    tm = min(MERGE_TM, seq)
    tiles_per_batch = seq // tm
    row = lambda i: (i, 0)
    const2 = lambda i: (0, 0)
    return pl.pallas_call(
        _merge_kernel,
        grid=(n // tm,),
        in_specs=[pl.BlockSpec((tm, d), row),
                  pl.BlockSpec((1, 6, d), lambda i: (i // tiles_per_batch, 0, 0)),
                  pl.BlockSpec((1, d), const2),
                  pl.BlockSpec((tm, DA_WIDTH), row),
                  pl.BlockSpec((tm, SG_WIDTH), row),
                  pl.BlockSpec(wg.shape, const2),
                  pl.BlockSpec(wa.shape, const2),
                  pl.BlockSpec(wb.shape, const2),
                  pl.BlockSpec(wo.shape, const2),
                  pl.BlockSpec((1, d), const2),
                  pl.BlockSpec(wq.shape, const2),
                  pl.BlockSpec(keys.shape, lambda i: (0, 0, 0))],
        out_specs=[pl.BlockSpec((tm, d), row),
                   pl.BlockSpec((tm, d), row),
                   pl.BlockSpec((2 * PEER_HEADS, PEER_KEYS, tm), lambda i: (0, 0, i))],
        out_shape=[jax.ShapeDtypeStruct((n, d), F32),
                   jax.ShapeDtypeStruct((n, d), F32),
                   jax.ShapeDtypeStruct((2 * PEER_HEADS, PEER_KEYS, n), F32)],
        compiler_params=pltpu.CompilerParams(vmem_limit_bytes=VMEM_LIMIT),
        name="merge",
    )(xf, mod6, norm1_g.reshape(1, d), ya, yb, wg, wa, wb, wo, norm2_g.reshape(1, d), wq, keys)


_CAND = [(i, j) for i in range(PEER_TOPK) for j in range(PEER_TOPK) if (i + 1) * (j + 1) <= PEER_TOPK]
_TOPK_CHUNK = 16


def _best_of(items):
    while len(items) > 1:
        nxt = []
        for a in range(0, len(items) - 1, 2):
            left, right = items[a], items[a + 1]
            keep = left[0] >= right[0]
            nxt.append(tuple(jnp.where(keep, x, y) for x, y in zip(left, right)))
        if len(items) % 2:
            nxt.append(items[-1])
        items = nxt
    return items[0]


def _topk_kernel(s_ref, e_ref, g_ref, w_ref, v_ref, i_ref):
    neg = jnp.float32(-jnp.inf)
    nkeys = w_ref.shape[1]
    w_ref[...] = s_ref[0]

    def body(it, prevs):
        outs = []
        for half in range(2):
            prev = prevs[half]
            partials = []
            for c0 in range(0, nkeys, _TOPK_CHUNK):
                items = []
                for k in range(c0, c0 + _TOPK_CHUNK):
                    s = jnp.where(prev == float(k), neg, w_ref[half, k])
                    w_ref[half, k] = s
                    items.append((s, float(k)))
                partials.append(_best_of(items))
            m, idx = _best_of(partials)
            v_ref[half, it] = m
            i_ref[half, it] = idx.astype(jnp.int32)
            outs.append(idx)
        return tuple(outs)

    none = jnp.full(w_ref.shape[2:], -1.0, F32)
    lax.fori_loop(0, PEER_TOPK, body, (none, none))

    cand = [v_ref[0, i] + v_ref[1, j] for (i, j) in _CAND]
    cid = [i_ref[0, i] * PEER_KEYS + i_ref[1, j] for (i, j) in _CAND]
    flat = [float(i * PEER_TOPK + j) for (i, j) in _CAND]
    tops = []
    pos = None
    for it in range(PEER_TOPK):
        if pos is not None:
            cand = [jnp.where(pos == f, neg, c) for c, f in zip(cand, flat)]
        m, pos, e = _best_of(list(zip(cand, flat, cid)))
        tops.append(m)
        e_ref[0, it] = e
    ex = [jnp.exp(t - tops[0]) for t in tops]
    tot = ex[0]
    for t in ex[1:]:
        tot = tot + t
    for it in range(PEER_TOPK):
        g_ref[0, it] = ex[it] / tot


def _topk(sct5):
    heads, _, keys, ng, _ = sct5.shape
    tg = min(TOPK_TG, ng)
    blk = (1, PEER_TOPK, tg, LANES)
    return pl.pallas_call(
        _topk_kernel,
        grid=(heads, ng // tg),
        in_specs=[pl.BlockSpec((1, 2, keys, tg, LANES), lambda h, t: (h, 0, 0, t, 0))],
        out_specs=[pl.BlockSpec(blk, lambda h, t: (h, 0, t, 0)),
                   pl.BlockSpec(blk, lambda h, t: (h, 0, t, 0))],
        out_shape=[jax.ShapeDtypeStruct((heads, PEER_TOPK, ng, LANES), jnp.int32),
                   jax.ShapeDtypeStruct((heads, PEER_TOPK, ng, LANES), F32)],
        scratch_shapes=[pltpu.VMEM((2, keys, tg, LANES), F32),
                        pltpu.VMEM((2, PEER_TOPK, tg, LANES), F32),
                        pltpu.VMEM((2, PEER_TOPK, tg, LANES), jnp.int32)],
        compiler_params=pltpu.CompilerParams(vmem_limit_bytes=VMEM_LIMIT),
        name="topk",
    )(sct5)


_NSLOT = 4
_AHEAD = 2
_BLOCK_PAIRS = PEER_TB * PEER_PAIRS


def _peer_kernel(idx01_ref, idxn_ref, tab_ref, gate_ref, h2_ref, x1_ref, g2_ref, fg_ref, o_ref,
                 buf0_ref, buf1_ref, buf2_ref, buf3_ref, wb_ref, sem_ref):
    bufs = (buf0_ref, buf1_ref, buf2_ref, buf3_ref)
    i = pl.program_id(0)
    nsteps = pl.num_programs(0)

    def issue(idx_ref, row0, s):
        for t in range(PEER_TB):
            issue_token(idx_ref, row0, t, s)

    def issue_token(idx_ref, row0, t, s):
        for j in range(PEER_PAIRS):
            pltpu.make_async_copy(tab_ref.at[idx_ref[row0 + t, j]], bufs[s].at[t * PEER_PAIRS + j],
                                  sem_ref.at[s]).start(priority=j % 2)

    def wait_block(s):
        pltpu.make_async_copy(bufs[s], bufs[s], sem_ref.at[s]).wait()

    @pl.when(i == 0)
    def _():
        for s in range(_AHEAD):
            issue(idx01_ref, s * PEER_TB, s)

    lane = lax.broadcasted_iota(jnp.int32, (SUBLANES, LANES), 1)
    g2 = g2_ref[0]
    fg = fg_ref[...]
    for s in range(_NSLOT):
        buf = bufs[s]
        nxt = (s + _AHEAD) % _NSLOT
        tok0 = s * PEER_TB
        wait_block(s)
        a_rows = []
        for t in range(PEER_TB):
            issue_token(idxn_ref, tok0, t, nxt)
            hv = h2_ref[tok0 + t]
            part = jnp.zeros((SUBLANES, LANES), F32)
            for j in range(PEER_PAIRS):
                prod = buf[t * PEER_PAIRS + j, 0:SUBLANES, :] * hv
                part = jnp.where(lane == j, jnp.sum(prod, axis=1, keepdims=True), part)
            a_rows.append(jnp.sum(part, axis=0, keepdims=True))
        a = jnp.concatenate(a_rows, axis=0)
        w = gate_ref[tok0:tok0 + PEER_TB, :] * jax.nn.gelu(a)
        for j in range(PEER_PAIRS):
            wb_ref[j] = jnp.broadcast_to(w[:, j:j + 1], (PEER_TB, LANES))
        for t in range(PEER_TB):
            y = jnp.zeros((SUBLANES, LANES), F32)
            for j in range(PEER_PAIRS):
                up = buf[t * PEER_PAIRS + j, SUBLANES:SLAB, :]
                y = y + up * jnp.broadcast_to(wb_ref[j, t:t + 1, :], (SUBLANES, LANES))
            x2 = x1_ref[tok0 + t] + g2 * y
            ms = jnp.sum(jnp.sum(x2 * x2, axis=1, keepdims=True), axis=0, keepdims=True)
            o_ref[tok0 + t] = x2 * lax.rsqrt(ms / (SUBLANES * LANES) + EPS) * fg

    @pl.when(i == nsteps - 1)
    def _():
        for s in range(_AHEAD):
            wait_block(s)


def _peer(eidx, gate, tab, h2r, x1r, g2r, fgr, seq):
    n = eidx.shape[0]
    tb = PEER_TB
    tstep = _NSLOT * tb
    nsteps = n // tstep
    assert n % tstep == 0 and seq % tstep == 0 and tb == SUBLANES
    steps_per_batch = seq // tstep
    eidx_ahead = jnp.concatenate([eidx[_AHEAD * tb:], eidx[n - _AHEAD * tb:]], axis=0)
    tok = lambda i: (i, 0, 0)
    slot_buf = pltpu.VMEM((_BLOCK_PAIRS, SLAB, LANES), F32)
    return pl.pallas_call(
        _peer_kernel,
        grid=(nsteps,),
        in_specs=[pl.BlockSpec((_AHEAD * tb, PEER_PAIRS), lambda i: (0, 0), memory_space=pltpu.SMEM),
                  pl.BlockSpec((tstep, PEER_PAIRS), lambda i: (i, 0), memory_space=pltpu.SMEM),
                  pl.BlockSpec(memory_space=pl.ANY),
                  pl.BlockSpec((tstep, PEER_PAIRS), lambda i: (i, 0)),
                  pl.BlockSpec((tstep, SUBLANES, LANES), tok),
                  pl.BlockSpec((tstep, SUBLANES, LANES), tok),
                  pl.BlockSpec((1, SUBLANES, LANES), lambda i: (i // steps_per_batch, 0, 0)),
                  pl.BlockSpec((SUBLANES, LANES), lambda i: (0, 0))],
        out_specs=pl.BlockSpec((tstep, SUBLANES, LANES), tok),
        out_shape=jax.ShapeDtypeStruct((n, SUBLANES, LANES), F32),
        scratch_shapes=[slot_buf] * _NSLOT + [pltpu.VMEM((PEER_PAIRS, tb, LANES), F32),
                                              pltpu.SemaphoreType.DMA((_NSLOT,))],
        compiler_params=pltpu.CompilerParams(dimension_semantics=("arbitrary",),
                                             vmem_limit_bytes=VMEM_LIMIT),
        name="peer",
    )(eidx, eidx_ahead, tab, gate, h2r, x1r, g2r, fgr)


def kernel(x, c, w_ada, b_ada, norm1_g, w_in, da_lambda_q1, da_lambda_k1, da_lambda_q2, da_lambda_k2, da_head_g, sg_ln_g, sg_ln_b, sg_w, sg_b, w_branch_a, w_branch_b, w_out, norm2_g, peer_w_query, peer_sub_keys, peer_down, peer_up, final_g):
    bsz, seq, d = x.shape
    n = bsz * seq
    assert w_ada.shape[0] == 1 and d == SUBLANES * LANES
    xf = x.reshape(n, d)

    mod6 = _ada(c, w_ada[0], b_ada[0]).reshape(bsz, 6, d)

    nproj = 2 * DA_QK_COLS + DA_WIDTH + 2 * SG_WIDTH
    w1 = w_in[0][:, :nproj].astype(BF16)
    wg = w_in[0][:, nproj:].astype(BF16)
    q, k, v, yb = _proj(xf, mod6, norm1_g[0], w1, sg_ln_g[0], sg_ln_b[0], sg_w[0], sg_b[0], seq)

    ya = _attn(q.reshape(bsz, seq, -1), k.reshape(bsz, seq, -1), v.reshape(bsz, seq, -1),
               da_lambda_q1[0], da_lambda_k1[0], da_lambda_q2[0], da_lambda_k2[0], da_head_g[0])

    keys = peer_sub_keys[0].reshape(2 * PEER_HEADS, PEER_KEYS, PEER_HALF).astype(BF16)
    x1, h2, sct = _merge(xf, mod6, norm1_g[0], ya.reshape(n, -1), yb, wg,
                         w_branch_a[0].astype(BF16), w_branch_b[0].astype(BF16),
                         w_out[0].astype(BF16), norm2_g[0], peer_w_query[0].astype(BF16), keys, seq)

    sct5 = sct.reshape(PEER_HEADS, 2, PEER_KEYS, n // LANES, LANES)
    eidx_t, gate = _topk(sct5)
    eidx = eidx_t.reshape(PEER_PAIRS, n).T
    gate = gate.reshape(PEER_PAIRS, n).T

    tab = jnp.concatenate([peer_down[0], peer_up[0]], axis=1).reshape(-1, SLAB, LANES)
    out = _peer(eidx, gate, tab,
                h2.reshape(n, SUBLANES, LANES), x1.reshape(n, SUBLANES, LANES),
                mod6[:, 5].reshape(bsz, SUBLANES, LANES), final_g.reshape(SUBLANES, LANES), seq)
    return out.reshape(bsz, seq, d)
```
